```python
import math
import jax, jax.numpy as jnp
from jax import lax
import numpy as np

D_MODEL = 2048
BATCH = 4
SEQ = 2048
DEPTH = 1

GRID_W = 64
CTX_LEN = 256
MIX_WIDTH = D_MODEL
POOL_WIDTH = MIX_WIDTH // 2
POOL_WINDOWS = (2, 4, 8, 16)
POOL_GROUP = POOL_WIDTH // len(POOL_WINDOWS)
ATTN_WIDTH = MIX_WIDTH - POOL_WIDTH
N_HEADS = 8
V_HEAD_DIM = ATTN_WIDTH // N_HEADS
QK_HEAD_DIM = V_HEAD_DIM // 2
ROPE_AXIS_DIM = QK_HEAD_DIM // 2
ROPE_BASE = 10000.0
D_FF = -(-8 * D_MODEL // 768) * 256
Q_BLOCK = 128
IN_WIDTH = POOL_WIDTH + 3 * ATTN_WIDTH
N_MOD = 6
EPS = 1e-6

kernel_name = "hybrid_pool_diffattn_dit_block"


def rmsnorm(x, g):
    xf = x.astype(jnp.float32)
    y = xf * lax.rsqrt(jnp.mean(xf * xf, axis=-1, keepdims=True) + EPS)
    return (y * g.astype(jnp.float32)).astype(x.dtype)


def axial_rope_tables(n_tokens):
    rows = n_tokens // GRID_W
    row = jnp.repeat(jnp.arange(rows), GRID_W)
    col = jnp.tile(jnp.arange(GRID_W), rows)
    half = ROPE_AXIS_DIM // 2
    inv = ROPE_BASE ** (-jnp.arange(half, dtype=jnp.float32) / half)
    ang_r = row.astype(jnp.float32)[:, None] * inv
    ang_c = col.astype(jnp.float32)[:, None] * inv
    return (jnp.cos(ang_r)[None, :, None, :], jnp.sin(ang_r)[None, :, None, :],
            jnp.cos(ang_c)[None, :, None, :], jnp.sin(ang_c)[None, :, None, :])


def _rotate(x, cos, sin):
    x1, x2 = jnp.split(x, 2, axis=-1)
    return jnp.concatenate([x1 * cos - x2 * sin, x1 * sin + x2 * cos], axis=-1)


def apply_axial_rope(x, tables):
    cr, sr, cc, sc = tables
    xf = x.astype(jnp.float32)
    out = jnp.concatenate([_rotate(xf[..., :ROPE_AXIS_DIM], cr, sr),
                           _rotate(xf[..., ROPE_AXIS_DIM:], cc, sc)], axis=-1)
    return out.astype(x.dtype)


def pool_mixer(u, w_pool_l, pool_scale_l):
    b, n = u.shape[0], u.shape[1]
    uf = u.astype(jnp.float32)
    cs = jnp.concatenate([jnp.zeros((b, 1, POOL_WIDTH), jnp.float32),
                          jnp.cumsum(uf, axis=1)], axis=1)
    t = jnp.arange(n)
    outs = []
    for g, w in enumerate(POOL_WINDOWS):
        lo = jnp.clip(t - w // 2, 0, n)
        hi = jnp.clip(t + w // 2, 0, n)
        sl = slice(g * POOL_GROUP, (g + 1) * POOL_GROUP)
        seg = cs[:, :, sl]
        mean = (seg[:, hi] - seg[:, lo]) / (hi - lo).astype(jnp.float32)[None, :, None]
        d = (mean - uf[:, :, sl]).astype(u.dtype)
        outs.append(jnp.einsum('blc,cd->bld', d, w_pool_l[g]))
    return jnp.concatenate(outs, axis=-1) * pool_scale_l


def project(h, w_in_l):
    b, n = h.shape[0], h.shape[1]
    proj = h @ w_in_l
    p = proj[..., :POOL_WIDTH]
    q = proj[..., POOL_WIDTH:POOL_WIDTH + ATTN_WIDTH].reshape(b, n, N_HEADS, 2, QK_HEAD_DIM)
    k = proj[..., POOL_WIDTH + ATTN_WIDTH:POOL_WIDTH + 2 * ATTN_WIDTH].reshape(b, n, N_HEADS, 2, QK_HEAD_DIM)
    v = proj[..., POOL_WIDTH + 2 * ATTN_WIDTH:].reshape(b, n, N_HEADS, V_HEAD_DIM)
    return p, q[..., 0, :], q[..., 1, :], k[..., 0, :], k[..., 1, :], v


def diff_attend(q1, q2, k1, k2, v, lam):
    scale = QK_HEAD_DIM ** -0.5
    s1 = jnp.einsum('bqhd,bkhd->bhqk', q1, k1).astype(jnp.float32) * scale
    s2 = jnp.einsum('bqhd,bkhd->bhqk', q2, k2).astype(jnp.float32) * scale
    p = jax.nn.softmax(s1, axis=-1) - lam * jax.nn.softmax(s2, axis=-1)
    return jnp.einsum('bhqk,bkhe->bqhe', p.astype(v.dtype), v)


def diff_attend_blocked(q1, q2, k1, k2, v, lam):
    b, n = q1.shape[0], q1.shape[1]
    nb = n // Q_BLOCK

    def to_blocks(q):
        return q.reshape(b, nb, Q_BLOCK, N_HEADS, QK_HEAD_DIM).swapaxes(0, 1)

    out = lax.map(lambda qs: diff_attend(qs[0], qs[1], k1, k2, v, lam),
                  (to_blocks(q1), to_blocks(q2)))
    return out.swapaxes(0, 1).reshape(b, n, N_HEADS, V_HEAD_DIM)


def swiglu(h, w_gate_up_l, w_down_l):
    gu = h @ w_gate_up_l
    g, u = jnp.split(gu, 2, axis=-1)
    return (jax.nn.silu(g) * u) @ w_down_l


def setup_inputs(seed: int = 0) -> dict:
    key = jax.random.key(seed)
    ks = jax.random.split(key, 24)
    f32 = jnp.float32

    def nrm(k, shape, scale):
        return jax.random.normal(k, shape, f32) * scale

    def gain(k, shape):
        return 1.0 + 0.02 * jax.random.normal(k, shape, f32)

    return {
        "x": nrm(ks[0], (BATCH, SEQ, D_MODEL), 1.0),
        "c": nrm(ks[1], (BATCH, D_MODEL), 1.0),
        "ctx": nrm(ks[2], (BATCH, CTX_LEN, D_MODEL), 1.0),
        "c_ctx": nrm(ks[3], (D_MODEL,), 1.0),
        "w_ada": nrm(ks[4], (DEPTH, D_MODEL, N_MOD * D_MODEL), D_MODEL ** -0.5),
        "b_ada": nrm(ks[5], (DEPTH, N_MOD * D_MODEL), 0.01),
        "g_pre_mix": gain(ks[6], (DEPTH, D_MODEL)),
        "g_post_mix": gain(ks[7], (DEPTH, D_MODEL)),
        "g_pre_ffn": gain(ks[8], (DEPTH, D_MODEL)),
        "g_post_ffn": gain(ks[9], (DEPTH, D_MODEL)),
        "w_in": nrm(ks[10], (DEPTH, D_MODEL, IN_WIDTH), D_MODEL ** -0.5),
        "w_pool": nrm(ks[11], (DEPTH, len(POOL_WINDOWS), POOL_GROUP, POOL_GROUP), POOL_GROUP ** -0.5),
        "pool_scale": gain(ks[12], (DEPTH, POOL_WIDTH)),
        "lambda_q1": nrm(ks[13], (DEPTH, QK_HEAD_DIM), 0.1),
        "lambda_k1": nrm(ks[14], (DEPTH, QK_HEAD_DIM), 0.1),
        "lambda_q2": nrm(ks[15], (DEPTH, QK_HEAD_DIM), 0.1),
        "lambda_k2": nrm(ks[16], (DEPTH, QK_HEAD_DIM), 0.1),
        "g_subln": gain(ks[17], (DEPTH, V_HEAD_DIM)),
        "w_out": nrm(ks[18], (DEPTH, MIX_WIDTH, D_MODEL), MIX_WIDTH ** -0.5),
        "w_gate_up": nrm(ks[19], (DEPTH, D_MODEL, 2 * D_FF), D_MODEL ** -0.5),
        "w_down": nrm(ks[20], (DEPTH, D_FF, D_MODEL), D_FF ** -0.5),
    }


def reference(x, c, ctx, c_ctx, w_ada, b_ada, g_pre_mix, g_post_mix, g_pre_ffn, g_post_ffn,
              w_in, w_pool, pool_scale, lambda_q1, lambda_k1, lambda_q2, lambda_k2, g_subln,
              w_out, w_gate_up, w_down):
    b, n = x.shape[0], x.shape[1]
    tables = axial_rope_tables(n)
    silu_c = jax.nn.silu(c)
    silu_cc = jax.nn.silu(c_ctx)

    for l in range(DEPTH):
        lambda_init = 0.8 - 0.6 * math.exp(-0.3 * l)
        lam = (jnp.exp(jnp.sum(lambda_q1[l].astype(jnp.float32) * lambda_k1[l].astype(jnp.float32)))
               - jnp.exp(jnp.sum(lambda_q2[l].astype(jnp.float32) * lambda_k2[l].astype(jnp.float32)))
               + lambda_init)

        mod = silu_c @ w_ada[l] + b_ada[l]
        sh_m, sc_m, gt_m, sh_f, sc_f, gt_f = jnp.split(mod[:, None, :], N_MOD, axis=-1)
        mod_c = silu_cc @ w_ada[l] + b_ada[l]
        shc_m, scc_m, gtc_m, shc_f, scc_f, gtc_f = jnp.split(mod_c, N_MOD, axis=-1)

        h = rmsnorm(x, g_pre_mix[l]) * (1 + sc_m) + sh_m
        hc = rmsnorm(ctx, g_pre_mix[l]) * (1 + scc_m) + shc_m
        p, q1, q2, k1, k2, v = project(h, w_in[l])
        pc, q1c, q2c, k1c, k2c, vc = project(hc, w_in[l])

        q1 = apply_axial_rope(q1, tables)
        q2 = apply_axial_rope(q2, tables)
        k1 = apply_axial_rope(k1, tables)
        k2 = apply_axial_rope(k2, tables)
        k1_all = jnp.concatenate([k1c, k1], axis=1)
        k2_all = jnp.concatenate([k2c, k2], axis=1)
        v_all = jnp.concatenate([vc, v], axis=1)

        attn = diff_attend_blocked(q1, q2, k1_all, k2_all, v_all, lam)
        attn = rmsnorm(attn, g_subln[l]) * (1 - lambda_init)
        pool = pool_mixer(p, w_pool[l], pool_scale[l])
        mix = jnp.concatenate([pool, attn.reshape(b, n, ATTN_WIDTH)], axis=-1) @ w_out[l]
        x_new = x + gt_m * rmsnorm(mix, g_post_mix[l])

        h2 = rmsnorm(x_new, g_pre_ffn[l]) * (1 + sc_f) + sh_f
        x_new = x_new + gt_f * rmsnorm(swiglu(h2, w_gate_up[l], w_down[l]), g_post_ffn[l])

        if l < DEPTH - 1:
            attn_c = diff_attend(q1c, q2c, k1c, k2c, vc, lam)
            attn_c = rmsnorm(attn_c, g_subln[l]) * (1 - lambda_init)
            pool_c = pool_mixer(pc, w_pool[l], pool_scale[l])
            mix_c = jnp.concatenate([pool_c, attn_c.reshape(b, ctx.shape[1], ATTN_WIDTH)], axis=-1) @ w_out[l]
            ctx = ctx + gtc_m * rmsnorm(mix_c, g_post_mix[l])
            h2c = rmsnorm(ctx, g_pre_ffn[l]) * (1 + scc_f) + shc_f
            ctx = ctx + gtc_f * rmsnorm(swiglu(h2c, w_gate_up[l], w_down[l]), g_post_ffn[l])

        x = x_new

    return x
```

```python
import functools
import math

import jax
import jax.numpy as jnp
import numpy as np
from jax import lax
from jax.experimental import pallas as pl
from jax.experimental.pallas import tpu as pltpu

D_MODEL = 2048
GRID_W = 64
POOL_WINDOWS = (2, 4, 8, 16)
POOL_WIDTH = D_MODEL // 2
POOL_GROUP = POOL_WIDTH // len(POOL_WINDOWS)
ATTN_WIDTH = D_MODEL - POOL_WIDTH
N_HEADS = 8
V_HEAD_DIM = ATTN_WIDTH // N_HEADS
QK_HEAD_DIM = V_HEAD_DIM // 2
ROPE_AXIS_DIM = QK_HEAD_DIM // 2
ROPE_BASE = 10000.0
N_MOD = 6
EPS = 1e-6
LAMBDA_INIT = 0.8 - 0.6 * math.exp(0.0)

V7X_LANES = 128
V7X_SUBLANES = 8
V7X_VMEM_BYTES = 64 * 1024 * 1024
V7X_VMEM_CAP = 56 * 1024 * 1024

F32 = jnp.float32
BF16 = jnp.bfloat16


def _vmem_limit(block_bytes):
    return int(min(V7X_VMEM_CAP, max(32 * 1024 * 1024, 2 * block_bytes)))


def _rms(x):
    return x * lax.rsqrt(jnp.mean(x * x, axis=-1, keepdims=True) + EPS)


def _silu(x):
    return x / (1.0 + jnp.exp(-x))


def _adaln_kernel(s_ref, w_ref, b_ref, o_ref):
    s = _silu(s_ref[...]).astype(BF16)
    o_ref[...] = jnp.dot(s, w_ref[...].astype(BF16), preferred_element_type=F32) + b_ref[...]


def _adaln(s8, w_ada, b_ada, tn=1024):
    d, n = w_ada.shape
    return pl.pallas_call(
        _adaln_kernel,
        grid=(n // tn,),
        in_specs=[pl.BlockSpec((8, d), lambda j: (0, 0)),
                  pl.BlockSpec((d, tn), lambda j: (0, j)),
                  pl.BlockSpec((1, tn), lambda j: (0, j))],
        out_specs=pl.BlockSpec((8, tn), lambda j: (0, j)),
        out_shape=jax.ShapeDtypeStruct((8, n), F32),
        compiler_params=pltpu.CompilerParams(
            dimension_semantics=("arbitrary",),
            vmem_limit_bytes=_vmem_limit(2 * d * tn * 4)),
        name="adaln",
    )(s8, w_ada, b_ada)


def _rope_tables(n_tokens):
    t = np.arange(n_tokens)
    half = ROPE_AXIS_DIM // 2
    inv = (np.float32(ROPE_BASE) ** (-np.arange(half, dtype=np.float32) / np.float32(half))).astype(np.float32)
    ang_r = (t // GRID_W).astype(np.float32)[:, None] * inv
    ang_c = (t % GRID_W).astype(np.float32)[:, None] * inv
    zero = np.zeros_like(ang_r)
    cos64 = np.concatenate([np.cos(ang_r), np.cos(ang_r), np.cos(ang_c), np.cos(ang_c)], axis=1)
    sa64 = np.concatenate([-np.sin(ang_r), zero, -np.sin(ang_c), zero], axis=1)
    sb64 = np.concatenate([zero, np.sin(ang_r), zero, np.sin(ang_c)], axis=1)
    rep = V7X_LANES // QK_HEAD_DIM
    return tuple(jnp.asarray(np.tile(a, (1, rep)).astype(np.float32)) for a in (cos64, sa64, sb64))


def _rope(x, cos, sa, sb):
    nxt = pltpu.roll(x, V7X_LANES - ROPE_AXIS_DIM // 2, 1)
    prv = pltpu.roll(x, ROPE_AXIS_DIM // 2, 1)
    return x * cos + nxt * sa + prv * sb


def _modulated_norm(x, g, sc, sh):
    return (_rms(x) * g) * (1.0 + sc) + sh


def _inproj_kernel(x_ref, sh_ref, sc_ref, g_ref, w_ref, cos_ref, sa_ref, sb_ref,
                   p_ref, q_ref, k_ref, v_ref):
    h = _modulated_norm(x_ref[...], g_ref[...], sc_ref[...], sh_ref[...]).astype(BF16)
    cos, sa, sb = cos_ref[...], sa_ref[...], sb_ref[...]
    n = POOL_WIDTH
    p_ref[...] = jnp.dot(h, w_ref[:, 0:n], preferred_element_type=F32)
    q = jnp.dot(h, w_ref[:, n:2 * n], preferred_element_type=F32)
    k = jnp.dot(h, w_ref[:, 2 * n:3 * n], preferred_element_type=F32)
    scale = QK_HEAD_DIM ** -0.5
    for c in range(n // V7X_LANES):
        sl = slice(c * V7X_LANES, (c + 1) * V7X_LANES)
        q_ref[:, sl] = (_rope(q[:, sl], cos, sa, sb) * scale).astype(BF16)
        k_ref[:, sl] = _rope(k[:, sl], cos, sa, sb).astype(BF16)
    v_ref[...] = jnp.dot(h, w_ref[:, 3 * n:4 * n], preferred_element_type=F32).astype(BF16)


def _inproj(x2, mod3, g_pre, w_in_bf, tables, seq, tm=512):
    m, d = x2.shape
    n = POOL_WIDTH
    per_b = seq // tm
    tok = lambda i: (i, 0)
    modspec = lambda chunk: pl.BlockSpec((None, 1, d), lambda i: (i // per_b, 0, chunk))
    tabspec = pl.BlockSpec((tm, V7X_LANES), lambda i: (i % per_b, 0))
    blk = tm * d * 4 * 2 + d * 4 * n * 2 + tm * n * (4 + 2 + 2 + 2) * 2
    return pl.pallas_call(
        _inproj_kernel,
        grid=(m // tm,),
        in_specs=[pl.BlockSpec((tm, d), tok), modspec(0), modspec(1),
                  pl.BlockSpec((1, d), lambda i: (0, 0)),
                  pl.BlockSpec((d, 4 * n), lambda i: (0, 0), pipeline_mode=pl.Buffered(1)),
                  tabspec, tabspec, tabspec],
        out_specs=[pl.BlockSpec((tm, n), tok)] * 4,
        out_shape=[jax.ShapeDtypeStruct((m, n), F32)] + [jax.ShapeDtypeStruct((m, n), BF16)] * 3,
        compiler_params=pltpu.CompilerParams(
            dimension_semantics=("arbitrary",), vmem_limit_bytes=_vmem_limit(blk)),
        name="inproj",
    )(x2, mod3, mod3, g_pre, w_in_bf, *tables)


def _ctxproj_kernel(x_ref, sh_ref, sc_ref, g_ref, w_ref, k_ref, v_ref):
    h = _modulated_norm(x_ref[...], g_ref[...], sc_ref[...], sh_ref[...]).astype(BF16)
    n = ATTN_WIDTH
    k_ref[...] = jnp.dot(h, w_ref[:, 0:n], preferred_element_type=F32).astype(BF16)
    v_ref[...] = jnp.dot(h, w_ref[:, n:2 * n], preferred_element_type=F32).astype(BF16)


def _ctxproj(c2, mod3, ctx_row, g_pre, w_in_bf, tm=512):
    m, d = c2.shape
    n = ATTN_WIDTH
    tok = lambda i: (i, 0)
    modspec = lambda chunk: pl.BlockSpec((None, 1, d), lambda i: (ctx_row, 0, chunk))
    blk = tm * d * 4 * 2 + d * 2 * n * 2 * 2 + tm * n * 2 * 2 * 2
    return pl.pallas_call(
        _ctxproj_kernel,
        grid=(m // tm,),
        in_specs=[pl.BlockSpec((tm, d), tok), modspec(0), modspec(1),
                  pl.BlockSpec((1, d), lambda i: (0, 0)),
                  pl.BlockSpec((d, 2 * n), lambda i: (0, 1))],
        out_specs=[pl.BlockSpec((tm, n), tok)] * 2,
        out_shape=[jax.ShapeDtypeStruct((m, n), BF16)] * 2,
        compiler_params=pltpu.CompilerParams(
            dimension_semantics=("arbitrary",), vmem_limit_bytes=_vmem_limit(blk)),
        name="ctxproj",
    )(c2, mod3, mod3, g_pre, w_in_bf)


def _attn_kernel(lam_ref, g_ref, q_ref, k_ref, kc_ref, v_ref, vc_ref, o_ref):
    lp = lam_ref[...]
    lam = (jnp.exp(jnp.sum(lp[0:1] * lp[1:2], axis=-1, keepdims=True))
           - jnp.exp(jnp.sum(lp[2:3] * lp[3:4], axis=-1, keepdims=True)) + LAMBDA_INIT)

    q = q_ref[...]
    lane = lax.broadcasted_iota(jnp.int32, q.shape, 1)
    zero = jnp.zeros_like(q)
    k, kc = k_ref[...], kc_ref[...]
    nt = (((1,), (1,)), ((), ()))

    def softmax_map(qm):
        s = lax.dot_general(qm, k, nt, preferred_element_type=F32)
        sc = lax.dot_general(qm, kc, nt, preferred_element_type=F32)
        m = jnp.maximum(jnp.max(s, axis=-1, keepdims=True), jnp.max(sc, axis=-1, keepdims=True))
        e, ec = jnp.exp(s - m), jnp.exp(sc - m)
        l = jnp.sum(e, axis=-1, keepdims=True) + jnp.sum(ec, axis=-1, keepdims=True)
        return e, ec, 1.0 / l

    e1, e1c, r1 = softmax_map(jnp.where(lane < QK_HEAD_DIM, q, zero))
    e2, e2c, r2 = softmax_map(jnp.where(lane >= QK_HEAD_DIM, q, zero))
    r2 = lam * r2
    p = (e1 * r1 - e2 * r2).astype(BF16)
    pc = (e1c * r1 - e2c * r2).astype(BF16)
    o = (jnp.dot(p, v_ref[...], preferred_element_type=F32)
         + jnp.dot(pc, vc_ref[...], preferred_element_type=F32))
    o_ref[...] = ((_rms(o) * g_ref[...]) * (1.0 - LAMBDA_INIT)).astype(BF16)


def _attn(lam_params, g_subln, q, k, kc, v, vc, batch, seq, ctx_len, tq=256):
    m = q.shape[0]
    nq = seq // tq
    hd = V_HEAD_DIM
    full = lambda shape: pl.BlockSpec(shape, lambda b, h, i: (0, 0))
    kv = lambda rows: pl.BlockSpec((rows, hd), lambda b, h, i: (b, h))
    qo = pl.BlockSpec((tq, hd), lambda b, h, i: (b * nq + i, h))
    blk = (2 * tq * hd * 2 + 2 * (seq + ctx_len) * hd * 2) * 2 + 6 * tq * (seq + ctx_len) * 4
    return pl.pallas_call(
        _attn_kernel,
        grid=(batch, N_HEADS, nq),
        in_specs=[full((4, QK_HEAD_DIM)), full((1, hd)), qo, kv(seq), kv(ctx_len), kv(seq), kv(ctx_len)],
        out_specs=qo,
        out_shape=jax.ShapeDtypeStruct((m, ATTN_WIDTH), BF16),
        compiler_params=pltpu.CompilerParams(
            dimension_semantics=("arbitrary",) * 3, vmem_limit_bytes=_vmem_limit(blk)),
        name="attn",
    )(lam_params, g_subln, q, k, kc, v, vc)


POOL_PAD = V7X_SUBLANES


def _pool_kernel(u_ref, w_ref, s_ref, o_ref, pad_ref):
    n = u_ref.shape[0]
    t = lax.broadcasted_iota(jnp.int32, (n, 1), 0)
    zeros = jnp.zeros((POOL_PAD, POOL_GROUP), F32)
    for g, win in enumerate(POOL_WINDOWS):
        cols = slice(g * POOL_GROUP, (g + 1) * POOL_GROUP)
        u = u_ref[:, cols]
        pad_ref[0:POOL_PAD, :] = zeros
        pad_ref[POOL_PAD:POOL_PAD + n, :] = u
        pad_ref[POOL_PAD + n:, :] = zeros
        acc = pad_ref[POOL_PAD - win // 2:POOL_PAD - win // 2 + n, :]
        for off in range(-win // 2 + 1, win // 2):
            acc = acc + pad_ref[POOL_PAD + off:POOL_PAD + off + n, :]
        cnt = (jnp.minimum(t + win // 2, n) - jnp.maximum(t - win // 2, 0)).astype(F32)
        d = (acc / cnt - u).astype(BF16)
        y = jnp.dot(d, w_ref[g].astype(BF16), preferred_element_type=F32)
        o_ref[:, cols] = (y * s_ref[:, cols]).astype(BF16)


def _pool(p, w_pool, pool_scale, batch, seq):
    m, n = p.shape
    blk = seq * n * (4 + 2) * 2 + w_pool.size * 4 * 2
    return pl.pallas_call(
        _pool_kernel,
        grid=(batch,),
        in_specs=[pl.BlockSpec((seq, n), lambda b: (b, 0)),
                  pl.BlockSpec(w_pool.shape, lambda b: (0, 0, 0)),
                  pl.BlockSpec((1, n), lambda b: (0, 0))],
        out_specs=pl.BlockSpec((seq, n), lambda b: (b, 0)),
        out_shape=jax.ShapeDtypeStruct((m, n), BF16),
        scratch_shapes=[pltpu.VMEM((seq + 2 * POOL_PAD, POOL_GROUP), F32)],
        compiler_params=pltpu.CompilerParams(
            dimension_semantics=("arbitrary",), vmem_limit_bytes=_vmem_limit(blk)),
        name="pool",
    )(p, w_pool, pool_scale)


def _outproj_kernel(pool_ref, attn_ref, w_ref, x_ref, gt_ref, sh_ref, sc_ref, gpost_ref, gpre_ref,
                    xn_ref, h2_ref):
    mix = (jnp.dot(pool_ref[...], w_ref[0:POOL_WIDTH, :], preferred_element_type=F32)
           + jnp.dot(attn_ref[...], w_ref[POOL_WIDTH:, :], preferred_element_type=F32))
    xn = x_ref[...] + gt_ref[...] * (_rms(mix) * gpost_ref[...])
    xn_ref[...] = xn
    h2_ref[...] = _modulated_norm(xn, gpre_ref[...], sc_ref[...], sh_ref[...]).astype(BF16)


def _outproj(pool_o, attn_o, w_out_bf, x2, mod3, g_post, g_pre_ffn, seq, tm=512):
    m, d = x2.shape
    per_b = seq // tm
    tok = lambda i: (i, 0)
    modspec = lambda chunk: pl.BlockSpec((None, 1, d), lambda i: (i // per_b, 0, chunk))
    vec = pl.BlockSpec((1, d), lambda i: (0, 0))
    blk = (tm * d * 2 + tm * d * 4 * 2 + tm * d * 2) * 2 + d * d * 2
    return pl.pallas_call(
        _outproj_kernel,
        grid=(m // tm,),
        in_specs=[pl.BlockSpec((tm, POOL_WIDTH), tok), pl.BlockSpec((tm, ATTN_WIDTH), tok),
                  pl.BlockSpec((d, d), lambda i: (0, 0), pipeline_mode=pl.Buffered(1)),
                  pl.BlockSpec((tm, d), tok), modspec(2), modspec(3), modspec(4), vec, vec],
        out_specs=[pl.BlockSpec((tm, d), tok)] * 2,
        out_shape=[jax.ShapeDtypeStruct((m, d), F32), jax.ShapeDtypeStruct((m, d), BF16)],
        compiler_params=pltpu.CompilerParams(
            dimension_semantics=("arbitrary",), vmem_limit_bytes=_vmem_limit(blk)),
        name="outproj",
    )(pool_o, attn_o, w_out_bf, x2, mod3, mod3, mod3, g_post, g_pre_ffn)


def _ffn_kernel(h_ref, wg_ref, wu_ref, wd_ref, x_ref, gt_ref, gpost_ref, o_ref, acc_ref):
    j = pl.program_id(1)
    h = h_ref[...]
    g = jnp.dot(h, wg_ref[...], preferred_element_type=F32)
    u = jnp.dot(h, wu_ref[...], preferred_element_type=F32)
    a = (_silu(g) * u).astype(BF16)
    part = jnp.dot(a, wd_ref[...], preferred_element_type=F32)

    @pl.when(j == 0)
    def _():
        acc_ref[...] = part

    @pl.when(j > 0)
    def _():
        acc_ref[...] += part

    @pl.when(j == pl.num_programs(1) - 1)
    def _():
        o_ref[...] = x_ref[...] + gt_ref[...] * (_rms(acc_ref[...]) * gpost_ref[...])


def _ffn(h2, w_gu_bf, w_down_bf, xn, mod3, g_post, seq, tm=512, tf=512):
    m, d = h2.shape
    d_ff = w_down_bf.shape[0]
    nf = d_ff // tf
    per_b = seq // tm
    tok = lambda i, j: (i, 0)
    blk = (tm * d * 2 + tm * d * 4 * 2 + 3 * d * tf * 2) * 2 + tm * d * 4
    return pl.pallas_call(
        _ffn_kernel,
        grid=(m // tm, nf),
        in_specs=[pl.BlockSpec((tm, d), tok),
                  pl.BlockSpec((d, tf), lambda i, j: (0, j)),
                  pl.BlockSpec((d, tf), lambda i, j: (0, j + nf)),
                  pl.BlockSpec((tf, d), lambda i, j: (j, 0)),
                  pl.BlockSpec((tm, d), tok),
                  pl.BlockSpec((None, 1, d), lambda i, j: (i // per_b, 0, 5)),
                  pl.BlockSpec((1, d), lambda i, j: (0, 0))],
        out_specs=pl.BlockSpec((tm, d), tok),
        out_shape=jax.ShapeDtypeStruct((m, d), F32),
        scratch_shapes=[pltpu.VMEM((tm, d), F32)],
        compiler_params=pltpu.CompilerParams(
            dimension_semantics=("arbitrary", "arbitrary"), vmem_limit_bytes=_vmem_limit(blk)),
        name="ffn",
    )(h2, w_gu_bf, w_gu_bf, w_down_bf, xn, mod3, g_post)


def kernel(x, c, ctx, c_ctx, w_ada, b_ada, g_pre_mix, g_post_mix, g_pre_ffn, g_post_ffn, w_in, w_pool,
           pool_scale, lambda_q1, lambda_k1, lambda_q2, lambda_k2, g_subln, w_out, w_gate_up, w_down):
    batch, seq, d = x.shape
    ctx_len = ctx.shape[1]
    assert w_ada.shape[0] == 1, "single-layer block: the context stream is never updated"
    assert d == D_MODEL and batch + 1 <= 8

    s8 = jnp.concatenate([c, c_ctx[None, :], jnp.zeros((8 - batch - 1, d), F32)], axis=0)
    mod = _adaln(s8, w_ada[0], b_ada[0][None, :])
    mod3 = mod.reshape(8, 1, N_MOD * d)

    w_in_bf = w_in[0].astype(BF16)
    w_out_bf = w_out[0].astype(BF16)
    w_gu_bf = w_gate_up[0].astype(BF16)
    w_down_bf = w_down[0].astype(BF16)

    x2 = x.reshape(batch * seq, d)
    c2 = ctx.reshape(batch * ctx_len, d)
    p, q, k, v = _inproj(x2, mod3, g_pre_mix, w_in_bf, _rope_tables(seq), seq)
    kc, vc = _ctxproj(c2, mod3, batch, g_pre_mix, w_in_bf)

    lam_params = jnp.concatenate([lambda_q1, lambda_k1, lambda_q2, lambda_k2], axis=0)
    attn_o = _attn(lam_params, g_subln, q, k, kc, v, vc, batch, seq, ctx_len)
    pool_o = _pool(p, w_pool[0], pool_scale, batch, seq)

    xn, h2 = _outproj(pool_o, attn_o, w_out_bf, x2, mod3, g_post_mix, g_pre_ffn, seq)
    out = _ffn(h2, w_gu_bf, w_down_bf, xn, mod3, g_post_ffn, seq)
    return out.reshape(batch, seq, d)
```

```python
import functools
import math

import jax
import jax.numpy as jnp
import numpy as np
from jax import lax
from jax.experimental import pallas as pl
from jax.experimental.pallas import tpu as pltpu

D_MODEL = 2048
GRID_W = 64
POOL_WINDOWS = (2, 4, 8, 16)
POOL_WIDTH = D_MODEL // 2
POOL_GROUP = POOL_WIDTH // len(POOL_WINDOWS)
ATTN_WIDTH = D_MODEL - POOL_WIDTH
N_HEADS = 8
V_HEAD_DIM = ATTN_WIDTH // N_HEADS
QK_HEAD_DIM = V_HEAD_DIM // 2
ROPE_AXIS_DIM = QK_HEAD_DIM // 2
ROPE_BASE = 10000.0
N_MOD = 6
EPS = 1e-6
LAMBDA_INIT = 0.8 - 0.6 * math.exp(0.0)

V7X_LANES = 128
V7X_SUBLANES = 8
V7X_VMEM_BYTES = 64 * 1024 * 1024
V7X_VMEM_CAP = 56 * 1024 * 1024

F32 = jnp.float32
BF16 = jnp.bfloat16


def _vmem_limit(block_bytes):
    return int(min(V7X_VMEM_CAP, max(32 * 1024 * 1024, 2 * block_bytes)))


def _rms(x):
    return x * lax.rsqrt(jnp.mean(x * x, axis=-1, keepdims=True) + EPS)


def _silu(x):
    return x / (1.0 + jnp.exp(-x))


def _adaln_kernel(s_ref, w_ref, b_ref, o_ref):
    s = _silu(s_ref[...]).astype(BF16)
    o_ref[...] = jnp.dot(s, w_ref[...].astype(BF16), preferred_element_type=F32) + b_ref[...]


def _adaln(s8, w_ada, b_ada, tn=1024):
    d, n = w_ada.shape
    return pl.pallas_call(
        _adaln_kernel,
        grid=(n // tn,),
        in_specs=[pl.BlockSpec((8, d), lambda j: (0, 0)),
                  pl.BlockSpec((d, tn), lambda j: (0, j)),
                  pl.BlockSpec((1, tn), lambda j: (0, j))],
        out_specs=pl.BlockSpec((8, tn), lambda j: (0, j)),
        out_shape=jax.ShapeDtypeStruct((8, n), F32),
        compiler_params=pltpu.CompilerParams(
            dimension_semantics=("arbitrary",),
            vmem_limit_bytes=_vmem_limit(2 * d * tn * 4)),
        name="adaln",
    )(s8, w_ada, b_ada)


def _rope_tables(n_tokens):
    t = np.arange(n_tokens)
    half = ROPE_AXIS_DIM // 2
    inv = (np.float32(ROPE_BASE) ** (-np.arange(half, dtype=np.float32) / np.float32(half))).astype(np.float32)
    ang_r = (t // GRID_W).astype(np.float32)[:, None] * inv
    ang_c = (t % GRID_W).astype(np.float32)[:, None] * inv
    zero = np.zeros_like(ang_r)
    cos64 = np.concatenate([np.cos(ang_r), np.cos(ang_r), np.cos(ang_c), np.cos(ang_c)], axis=1)
    sa64 = np.concatenate([-np.sin(ang_r), zero, -np.sin(ang_c), zero], axis=1)
    sb64 = np.concatenate([zero, np.sin(ang_r), zero, np.sin(ang_c)], axis=1)
    rep = V7X_LANES // QK_HEAD_DIM
    return tuple(jnp.asarray(np.tile(a, (1, rep)).astype(np.float32)) for a in (cos64, sa64, sb64))


def _rope(x, cos, sa, sb):
    nxt = pltpu.roll(x, V7X_LANES - ROPE_AXIS_DIM // 2, 1)
    prv = pltpu.roll(x, ROPE_AXIS_DIM // 2, 1)
    return x * cos + nxt * sa + prv * sb


def _modulated_norm(x, g, sc, sh):
    return (_rms(x) * g) * (1.0 + sc) + sh


NT_DIMS = (((1,), (1,)), ((), ()))
SUM_ROWS = 2 * V7X_SUBLANES
LOG2_E = math.log2(math.e)


def _inproj_kernel(x_ref, sh_ref, sc_ref, g_ref, w_ref, wvt_ref, cos_ref, sa_ref, sb_ref,
                   p_ref, q_ref, k_ref, vt_ref):
    h = _modulated_norm(x_ref[...], g_ref[...], sc_ref[...], sh_ref[...]).astype(BF16)
    cos, sa, sb = cos_ref[...], sa_ref[...], sb_ref[...]
    n = POOL_WIDTH
    p_ref[...] = jnp.dot(h, w_ref[:, 0:n], preferred_element_type=F32)
    q = jnp.dot(h, w_ref[:, n:2 * n], preferred_element_type=F32)
    k = jnp.dot(h, w_ref[:, 2 * n:3 * n], preferred_element_type=F32)
    scale = QK_HEAD_DIM ** -0.5 * LOG2_E
    for c in range(n // V7X_LANES):
        sl = slice(c * V7X_LANES, (c + 1) * V7X_LANES)
        q_ref[:, sl] = (_rope(q[:, sl], cos, sa, sb) * scale).astype(BF16)
        k_ref[:, sl] = _rope(k[:, sl], cos, sa, sb).astype(BF16)
    vt_ref[...] = lax.dot_general(wvt_ref[...], h, NT_DIMS, preferred_element_type=F32).astype(BF16)


def _inproj(x2, mod3, g_pre, w_in_bf, w_vt_bf, tables, seq, tm=512):
    m, d = x2.shape
    n = POOL_WIDTH
    per_b = seq // tm
    tok = lambda i: (i, 0)
    modspec = lambda chunk: pl.BlockSpec((None, 1, d), lambda i: (i // per_b, 0, chunk))
    tabspec = pl.BlockSpec((tm, V7X_LANES), lambda i: (i % per_b, 0))
    blk = tm * d * 4 * 2 + d * 4 * n * 2 + tm * n * (4 + 2 + 2 + 2) * 2
    return pl.pallas_call(
        _inproj_kernel,
        grid=(m // tm,),
        in_specs=[pl.BlockSpec((tm, d), tok), modspec(0), modspec(1),
                  pl.BlockSpec((1, d), lambda i: (0, 0)),
                  pl.BlockSpec((d, 3 * n), lambda i: (0, 0), pipeline_mode=pl.Buffered(1)),
                  pl.BlockSpec((n, d), lambda i: (0, 0), pipeline_mode=pl.Buffered(1)),
                  tabspec, tabspec, tabspec],
        out_specs=[pl.BlockSpec((tm, n), tok)] * 3 + [pl.BlockSpec((n, tm), lambda i: (0, i))],
        out_shape=[jax.ShapeDtypeStruct((m, n), F32)] + [jax.ShapeDtypeStruct((m, n), BF16)] * 2
                  + [jax.ShapeDtypeStruct((n, m), BF16)],
        compiler_params=pltpu.CompilerParams(
            dimension_semantics=("arbitrary",), vmem_limit_bytes=_vmem_limit(blk)),
        name="inproj",
    )(x2, mod3, mod3, g_pre, w_in_bf, w_vt_bf, *tables)


def _ctxproj_kernel(x_ref, sh_ref, sc_ref, g_ref, wk_ref, wvt_ref, k_ref, vt_ref):
    h = _modulated_norm(x_ref[...], g_ref[...], sc_ref[...], sh_ref[...]).astype(BF16)
    k_ref[...] = jnp.dot(h, wk_ref[...], preferred_element_type=F32).astype(BF16)
    vt_ref[...] = lax.dot_general(wvt_ref[...], h, NT_DIMS, preferred_element_type=F32).astype(BF16)


def _ctxproj(c2, mod3, ctx_row, g_pre, w_in_bf, w_vt_bf, tm=512):
    m, d = c2.shape
    n = ATTN_WIDTH
    tok = lambda i: (i, 0)
    modspec = lambda chunk: pl.BlockSpec((None, 1, d), lambda i: (ctx_row, 0, chunk))
    blk = tm * d * 4 * 2 + d * 2 * n * 2 * 2 + tm * n * 2 * 2 * 2
    return pl.pallas_call(
        _ctxproj_kernel,
        grid=(m // tm,),
        in_specs=[pl.BlockSpec((tm, d), tok), modspec(0), modspec(1),
                  pl.BlockSpec((1, d), lambda i: (0, 0)),
                  pl.BlockSpec((d, n), lambda i: (0, 2)),
                  pl.BlockSpec((n, d), lambda i: (0, 0))],
        out_specs=[pl.BlockSpec((tm, n), tok), pl.BlockSpec((n, tm), lambda i: (0, i))],
        out_shape=[jax.ShapeDtypeStruct((m, n), BF16), jax.ShapeDtypeStruct((n, m), BF16)],
        compiler_params=pltpu.CompilerParams(
            dimension_semantics=("arbitrary",), vmem_limit_bytes=_vmem_limit(blk)),
        name="ctxproj",
    )(c2, mod3, mod3, g_pre, w_in_bf, w_vt_bf)


def _attn_kernel(lam_ref, g_ref, q_ref, k_ref, kc_ref, vt_ref, vct_ref, o_ref, e1_ref, e2_ref, *, tq):
    lp = lam_ref[...]
    lam = (jnp.exp(jnp.sum(lp[0:1] * lp[1:2], axis=-1, keepdims=True))
           - jnp.exp(jnp.sum(lp[2:3] * lp[3:4], axis=-1, keepdims=True)) + LAMBDA_INIT)
    k, kc = k_ref[...], kc_ref[...]
    vt = jnp.concatenate([vt_ref[...], vct_ref[...]], axis=1)
    vt = jnp.concatenate([vt, jnp.ones((SUM_ROWS, vt.shape[1]), BF16)], axis=0)
    gain = g_ref[...] * (1.0 - LAMBDA_INIT)

    def scores(i):
        q = q_ref[i * tq:(i + 1) * tq, :]
        lane = lax.broadcasted_iota(jnp.int32, q.shape, 1)
        zero = jnp.zeros_like(q)
        out = []
        for qm in (jnp.where(lane < QK_HEAD_DIM, q, zero), jnp.where(lane >= QK_HEAD_DIM, q, zero)):
            out.append((lax.dot_general(k, qm, NT_DIMS, preferred_element_type=F32),
                        lax.dot_general(kc, qm, NT_DIMS, preferred_element_type=F32)))
        return out

    n_lat = k_ref.shape[0]

    def weighted_values(e_ref, s, sc):
        m = jnp.maximum(jnp.max(s, axis=0, keepdims=True), jnp.max(sc, axis=0, keepdims=True))
        e_ref[0:n_lat, :] = jnp.exp2(s - m).astype(BF16)
        e_ref[n_lat:, :] = jnp.exp2(sc - m).astype(BF16)
        o = jnp.dot(vt, e_ref[...], preferred_element_type=F32)
        return o[:V_HEAD_DIM], o[V_HEAD_DIM:V_HEAD_DIM + 1]

    n_tiles = q_ref.shape[0] // tq
    nxt = scores(0)
    for i in range(n_tiles):
        cur = nxt
        if i + 1 < n_tiles:
            nxt = scores(i + 1)
        (o1, l1), (o2, l2) = weighted_values(e1_ref, *cur[0]), weighted_values(e2_ref, *cur[1])
        o = o1 * (1.0 / l1) - o2 * (lam / l2)
        o = o * lax.rsqrt(jnp.mean(o * o, axis=0, keepdims=True) + EPS)
        o_ref[i * tq:(i + 1) * tq, :] = (o.T * gain).astype(BF16)


def _attn(lam_params, g_subln, q, k, kc, vt, vct, batch, seq, ctx_len, tq=512):
    m = q.shape[0]
    hd = V_HEAD_DIM
    full = lambda shape: pl.BlockSpec(shape, lambda b, h: (0, 0))
    rows = lambda n: pl.BlockSpec((n, hd), lambda b, h: (b, h))
    cols = lambda n: pl.BlockSpec((hd, n), lambda b, h: (h, b))
    blk = (2 * seq + 2 * (seq + ctx_len)) * hd * 2 * 2 + 6 * tq * (seq + ctx_len) * 4
    return pl.pallas_call(
        functools.partial(_attn_kernel, tq=tq),
        grid=(batch, N_HEADS),
        in_specs=[full((4, QK_HEAD_DIM)), full((1, hd)), rows(seq), rows(seq), rows(ctx_len),
                  cols(seq), cols(ctx_len)],
        out_specs=rows(seq),
        out_shape=jax.ShapeDtypeStruct((m, ATTN_WIDTH), BF16),
        scratch_shapes=[pltpu.VMEM((seq + ctx_len, tq), BF16)] * 2,
        compiler_params=pltpu.CompilerParams(
            dimension_semantics=("arbitrary",) * 2, vmem_limit_bytes=_vmem_limit(blk)),
        name="attn",
    )(lam_params, g_subln, q, k, kc, vt, vct)


POOL_PAD = V7X_SUBLANES


def _pool_kernel(u_ref, w_ref, s_ref, o_ref, pad_ref):
    n = u_ref.shape[0]
    t = lax.broadcasted_iota(jnp.int32, (n, 1), 0)
    zeros = jnp.zeros((POOL_PAD, POOL_GROUP), F32)
    for g, win in enumerate(POOL_WINDOWS):
        cols = slice(g * POOL_GROUP, (g + 1) * POOL_GROUP)
        u = u_ref[:, cols]
        pad_ref[0:POOL_PAD, :] = zeros
        pad_ref[POOL_PAD:POOL_PAD + n, :] = u
        pad_ref[POOL_PAD + n:, :] = zeros
        acc = pad_ref[POOL_PAD - win // 2:POOL_PAD - win // 2 + n, :]
        for off in range(-win // 2 + 1, win // 2):
            acc = acc + pad_ref[POOL_PAD + off:POOL_PAD + off + n, :]
        cnt = (jnp.minimum(t + win // 2, n) - jnp.maximum(t - win // 2, 0)).astype(F32)
        d = (acc / cnt - u).astype(BF16)
        y = jnp.dot(d, w_ref[g].astype(BF16), preferred_element_type=F32)
        o_ref[:, cols] = (y * s_ref[:, cols]).astype(BF16)


def _pool(p, w_pool, pool_scale, batch, seq):
    m, n = p.shape
    blk = seq * n * (4 + 2) * 2 + w_pool.size * 4 * 2
    return pl.pallas_call(
        _pool_kernel,
        grid=(batch,),
        in_specs=[pl.BlockSpec((seq, n), lambda b: (b, 0)),
                  pl.BlockSpec(w_pool.shape, lambda b: (0, 0, 0)),
                  pl.BlockSpec((1, n), lambda b: (0, 0))],
        out_specs=pl.BlockSpec((seq, n), lambda b: (b, 0)),
        out_shape=jax.ShapeDtypeStruct((m, n), BF16),
        scratch_shapes=[pltpu.VMEM((seq + 2 * POOL_PAD, POOL_GROUP), F32)],
        compiler_params=pltpu.CompilerParams(
            dimension_semantics=("arbitrary",), vmem_limit_bytes=_vmem_limit(blk)),
        name="pool",
    )(p, w_pool, pool_scale)


def _outproj_kernel(pool_ref, attn_ref, w_ref, x_ref, gt_ref, sh_ref, sc_ref, gpost_ref, gpre_ref,
                    xn_ref, h2_ref):
    mix = (jnp.dot(pool_ref[...], w_ref[0:POOL_WIDTH, :], preferred_element_type=F32)
           + jnp.dot(attn_ref[...], w_ref[POOL_WIDTH:, :], preferred_element_type=F32))
    xn = x_ref[...] + gt_ref[...] * (_rms(mix) * gpost_ref[...])
    xn_ref[...] = xn
    h2_ref[...] = _modulated_norm(xn, gpre_ref[...], sc_ref[...], sh_ref[...]).astype(BF16)


def _outproj(pool_o, attn_o, w_out_bf, x2, mod3, g_post, g_pre_ffn, seq, tm=512):
    m, d = x2.shape
    per_b = seq // tm
    tok = lambda i: (i, 0)
    modspec = lambda chunk: pl.BlockSpec((None, 1, d), lambda i: (i // per_b, 0, chunk))
    vec = pl.BlockSpec((1, d), lambda i: (0, 0))
    blk = (tm * d * 2 + tm * d * 4 * 2 + tm * d * 2) * 2 + d * d * 2
    return pl.pallas_call(
        _outproj_kernel,
        grid=(m // tm,),
        in_specs=[pl.BlockSpec((tm, POOL_WIDTH), tok), pl.BlockSpec((tm, ATTN_WIDTH), tok),
                  pl.BlockSpec((d, d), lambda i: (0, 0), pipeline_mode=pl.Buffered(1)),
                  pl.BlockSpec((tm, d), tok), modspec(2), modspec(3), modspec(4), vec, vec],
        out_specs=[pl.BlockSpec((tm, d), tok)] * 2,
        out_shape=[jax.ShapeDtypeStruct((m, d), F32), jax.ShapeDtypeStruct((m, d), BF16)],
        compiler_params=pltpu.CompilerParams(
            dimension_semantics=("arbitrary",), vmem_limit_bytes=_vmem_limit(blk)),
        name="outproj",
    )(pool_o, attn_o, w_out_bf, x2, mod3, mod3, mod3, g_post, g_pre_ffn)


def _ffn_kernel(h_ref, wg_ref, wu_ref, wd_ref, x_ref, gt_ref, gpost_ref, o_ref, acc_ref):
    j = pl.program_id(1)

    @pl.when(j == 0)
    def _():
        acc_ref[...] = jnp.zeros_like(acc_ref)

    h = h_ref[...]
    g = jnp.dot(h, wg_ref[...], preferred_element_type=F32)
    u = jnp.dot(h, wu_ref[...], preferred_element_type=F32)
    a = (_silu(g) * u).astype(BF16)
    acc_ref[...] += jnp.dot(a, wd_ref[...], preferred_element_type=F32)

    @pl.when(j == pl.num_programs(1) - 1)
    def _():
        o_ref[...] = x_ref[...] + gt_ref[...] * (_rms(acc_ref[...]) * gpost_ref[...])


def _ffn(h2, w_gu_bf, w_down_bf, xn, mod3, g_post, seq, tm=512, tf=512):
    m, d = h2.shape
    d_ff = w_down_bf.shape[0]
    nf = d_ff // tf
    per_b = seq // tm
    tok = lambda i, j: (i, 0)
    blk = (tm * d * 2 + tm * d * 4 * 2 + 3 * d * tf * 2) * 2 + tm * d * 4
    return pl.pallas_call(
        _ffn_kernel,
        grid=(m // tm, nf),
        in_specs=[pl.BlockSpec((tm, d), tok),
                  pl.BlockSpec((d, tf), lambda i, j: (0, j)),
                  pl.BlockSpec((d, tf), lambda i, j: (0, j + nf)),
                  pl.BlockSpec((tf, d), lambda i, j: (j, 0)),
                  pl.BlockSpec((tm, d), tok),
                  pl.BlockSpec((None, 1, d), lambda i, j: (i // per_b, 0, 5)),
                  pl.BlockSpec((1, d), lambda i, j: (0, 0))],
        out_specs=pl.BlockSpec((tm, d), tok),
        out_shape=jax.ShapeDtypeStruct((m, d), F32),
        scratch_shapes=[pltpu.VMEM((tm, d), F32)],
        compiler_params=pltpu.CompilerParams(
            dimension_semantics=("arbitrary", "arbitrary"), vmem_limit_bytes=_vmem_limit(blk)),
        name="ffn",
    )(h2, w_gu_bf, w_gu_bf, w_down_bf, xn, mod3, g_post)


def kernel(x, c, ctx, c_ctx, w_ada, b_ada, g_pre_mix, g_post_mix, g_pre_ffn, g_post_ffn, w_in, w_pool,
           pool_scale, lambda_q1, lambda_k1, lambda_q2, lambda_k2, g_subln, w_out, w_gate_up, w_down):
    batch, seq, d = x.shape
    ctx_len = ctx.shape[1]
    assert w_ada.shape[0] == 1, "single-layer block: the context stream is never updated"
    assert d == D_MODEL and batch + 1 <= 8

    s8 = jnp.concatenate([c, c_ctx[None, :], jnp.zeros((8 - batch - 1, d), F32)], axis=0)
    mod = _adaln(s8, w_ada[0], b_ada[0][None, :])
    mod3 = mod.reshape(8, 1, N_MOD * d)

    w_in_bf = w_in[0].astype(BF16)
    w_vt_bf = w_in[0][:, POOL_WIDTH + 2 * ATTN_WIDTH:].T.astype(BF16)
    w_out_bf = w_out[0].astype(BF16)
    w_gu_bf = w_gate_up[0].astype(BF16)
    w_down_bf = w_down[0].astype(BF16)

    x2 = x.reshape(batch * seq, d)
    c2 = ctx.reshape(batch * ctx_len, d)
    p, q, k, vt = _inproj(x2, mod3, g_pre_mix, w_in_bf, w_vt_bf, _rope_tables(seq), seq)
    kc, vct = _ctxproj(c2, mod3, batch, g_pre_mix, w_in_bf, w_vt_bf)

    lam_params = jnp.concatenate([lambda_q1, lambda_k1, lambda_q2, lambda_k2], axis=0)
    attn_o = _attn(lam_params, g_subln, q, k, kc, vt, vct, batch, seq, ctx_len)
    pool_o = _pool(p, w_pool[0], pool_scale, batch, seq)

    xn, h2 = _outproj(pool_o, attn_o, w_out_bf, x2, mod3, g_post_mix, g_pre_ffn, seq)
    out = _ffn(h2, w_gu_bf, w_down_bf, xn, mod3, g_post_ffn, seq)
    return out.reshape(batch, seq, d)
```

```python
import functools
import math

import jax
import jax.numpy as jnp
import numpy as np
from jax import lax
from jax.experimental import pallas as pl
from jax.experimental.pallas import tpu as pltpu

D_MODEL = 2048
GRID_W = 64
POOL_WINDOWS = (2, 4, 8, 16)
POOL_WIDTH = D_MODEL // 2
POOL_GROUP = POOL_WIDTH // len(POOL_WINDOWS)
ATTN_WIDTH = D_MODEL - POOL_WIDTH
N_HEADS = 8
V_HEAD_DIM = ATTN_WIDTH // N_HEADS
QK_HEAD_DIM = V_HEAD_DIM // 2
ROPE_AXIS_DIM = QK_HEAD_DIM // 2
ROPE_BASE = 10000.0
N_MOD = 6
EPS = 1e-6
LAMBDA_INIT = 0.8 - 0.6 * math.exp(0.0)

V7X_LANES = 128
V7X_SUBLANES = 8
V7X_VMEM_BYTES = 64 * 1024 * 1024
V7X_VMEM_CAP = 56 * 1024 * 1024

F32 = jnp.float32
BF16 = jnp.bfloat16


def _vmem_limit(block_bytes):
    return int(min(V7X_VMEM_CAP, max(32 * 1024 * 1024, 2 * block_bytes)))


def _rms(x):
    return x * lax.rsqrt(jnp.mean(x * x, axis=-1, keepdims=True) + EPS)


def _silu(x):
    return x / (1.0 + jnp.exp(-x))


def _adaln_kernel(s_ref, w_ref, b_ref, o_ref):
    s = _silu(s_ref[...]).astype(BF16)
    o_ref[...] = jnp.dot(s, w_ref[...].astype(BF16), preferred_element_type=F32) + b_ref[...]


def _adaln(s8, w_ada, b_ada, tn=1024):
    d, n = w_ada.shape
    return pl.pallas_call(
        _adaln_kernel,
        grid=(n // tn,),
        in_specs=[pl.BlockSpec((8, d), lambda j: (0, 0)),
                  pl.BlockSpec((d, tn), lambda j: (0, j)),
                  pl.BlockSpec((1, tn), lambda j: (0, j))],
        out_specs=pl.BlockSpec((8, tn), lambda j: (0, j)),
        out_shape=jax.ShapeDtypeStruct((8, n), F32),
        compiler_params=pltpu.CompilerParams(
            dimension_semantics=("arbitrary",),
            vmem_limit_bytes=_vmem_limit(2 * d * tn * 4)),
        name="adaln",
    )(s8, w_ada, b_ada)


def _wprep_kernel(w_ref, wbf_ref, wvt_ref, *, n_plain):
    j = pl.program_id(0)

    @pl.when(j < n_plain)
    def _():
        wbf_ref[...] = w_ref[...].astype(BF16)

    @pl.when(j >= n_plain)
    def _():
        wvt_ref[...] = w_ref[...].T.astype(BF16)


def _wprep(w_in, n_value, tn=512):
    d, n = w_in.shape
    n_plain = (n - n_value) // tn
    return pl.pallas_call(
        functools.partial(_wprep_kernel, n_plain=n_plain),
        grid=(n // tn,),
        in_specs=[pl.BlockSpec((d, tn), lambda j: (0, j))],
        out_specs=[pl.BlockSpec((d, tn), lambda j: (0, jnp.minimum(j, n_plain - 1))),
                   pl.BlockSpec((tn, d), lambda j: (jnp.maximum(j - n_plain, 0), 0))],
        out_shape=[jax.ShapeDtypeStruct((d, n - n_value), BF16), jax.ShapeDtypeStruct((n_value, d), BF16)],
        compiler_params=pltpu.CompilerParams(
            dimension_semantics=("arbitrary",), vmem_limit_bytes=_vmem_limit(d * tn * (4 + 2 + 2) * 2)),
        name="wprep",
    )(w_in)


def _rope_tables(n_tokens):
    t = np.arange(n_tokens)
    half = ROPE_AXIS_DIM // 2
    inv = (np.float32(ROPE_BASE) ** (-np.arange(half, dtype=np.float32) / np.float32(half))).astype(np.float32)
    ang_r = (t // GRID_W).astype(np.float32)[:, None] * inv
    ang_c = (t % GRID_W).astype(np.float32)[:, None] * inv
    zero = np.zeros_like(ang_r)
    cos64 = np.concatenate([np.cos(ang_r), np.cos(ang_r), np.cos(ang_c), np.cos(ang_c)], axis=1)
    sa64 = np.concatenate([-np.sin(ang_r), zero, -np.sin(ang_c), zero], axis=1)
    sb64 = np.concatenate([zero, np.sin(ang_r), zero, np.sin(ang_c)], axis=1)
    rep = V7X_LANES // QK_HEAD_DIM
    return tuple(jnp.asarray(np.tile(a, (1, rep)).astype(np.float32)) for a in (cos64, sa64, sb64))


def _rope(x, cos, sa, sb):
    nxt = pltpu.roll(x, V7X_LANES - ROPE_AXIS_DIM // 2, 1)
    prv = pltpu.roll(x, ROPE_AXIS_DIM // 2, 1)
    return x * cos + nxt * sa + prv * sb


def _modulated_norm(x, g, sc, sh):
    return (_rms(x) * g) * (1.0 + sc) + sh


NT_DIMS = (((1,), (1,)), ((), ()))
BF16_TILE_ROWS = 2 * V7X_SUBLANES
SUM_ROWS = BF16_TILE_ROWS
LOG2_E = math.log2(math.e)


def _inproj_kernel(x_ref, sh_ref, sc_ref, g_ref, w_ref, wvt_ref, cos_ref, sa_ref, sb_ref,
                   p_ref, q_ref, k_ref, vt_ref):
    h = _modulated_norm(x_ref[...], g_ref[...], sc_ref[...], sh_ref[...]).astype(BF16)
    cos, sa, sb = cos_ref[...], sa_ref[...], sb_ref[...]
    n = POOL_WIDTH
    p_ref[...] = jnp.dot(h, w_ref[:, 0:n], preferred_element_type=F32)
    q = jnp.dot(h, w_ref[:, n:2 * n], preferred_element_type=F32)
    k = jnp.dot(h, w_ref[:, 2 * n:3 * n], preferred_element_type=F32)
    scale = QK_HEAD_DIM ** -0.5 * LOG2_E
    for c in range(n // V7X_LANES):
        sl = slice(c * V7X_LANES, (c + 1) * V7X_LANES)
        q_ref[:, sl] = (_rope(q[:, sl], cos, sa, sb) * scale).astype(BF16)
        k_ref[:, sl] = _rope(k[:, sl], cos, sa, sb).astype(BF16)
    vt_ref[...] = lax.dot_general(wvt_ref[...], h, NT_DIMS, preferred_element_type=F32).astype(BF16)


def _inproj(x2, mod3, g_pre, w_in_bf, w_vt_bf, tables, seq, tm=512):
    m, d = x2.shape
    n = POOL_WIDTH
    per_b = seq // tm
    tok = lambda i: (i, 0)
    modspec = lambda chunk: pl.BlockSpec((None, 1, d), lambda i: (i // per_b, 0, chunk))
    tabspec = pl.BlockSpec((tm, V7X_LANES), lambda i: (i % per_b, 0))
    blk = tm * d * 4 * 2 + d * 4 * n * 2 + tm * n * (4 + 2 + 2 + 2) * 2
    return pl.pallas_call(
        _inproj_kernel,
        grid=(m // tm,),
        in_specs=[pl.BlockSpec((tm, d), tok), modspec(0), modspec(1),
                  pl.BlockSpec((1, d), lambda i: (0, 0)),
                  pl.BlockSpec((d, 3 * n), lambda i: (0, 0), pipeline_mode=pl.Buffered(1)),
                  pl.BlockSpec((n, d), lambda i: (0, 0), pipeline_mode=pl.Buffered(1)),
                  tabspec, tabspec, tabspec],
        out_specs=[pl.BlockSpec((tm, n), tok)] * 3 + [pl.BlockSpec((n, tm), lambda i: (0, i))],
        out_shape=[jax.ShapeDtypeStruct((m, n), F32)] + [jax.ShapeDtypeStruct((m, n), BF16)] * 2
                  + [jax.ShapeDtypeStruct((n, m), BF16)],
        compiler_params=pltpu.CompilerParams(
            dimension_semantics=("arbitrary",), vmem_limit_bytes=_vmem_limit(blk)),
        name="inproj",
    )(x2, mod3, mod3, g_pre, w_in_bf, w_vt_bf, *tables)


def _ctxproj_kernel(x_ref, sh_ref, sc_ref, g_ref, wk_ref, wvt_ref, k_ref, vt_ref):
    h = _modulated_norm(x_ref[...], g_ref[...], sc_ref[...], sh_ref[...]).astype(BF16)
    k_ref[...] = jnp.dot(h, wk_ref[...], preferred_element_type=F32).astype(BF16)
    vt_ref[...] = lax.dot_general(wvt_ref[...], h, NT_DIMS, preferred_element_type=F32).astype(BF16)


def _ctxproj(c2, mod3, ctx_row, g_pre, w_in_bf, w_vt_bf, tm=512):
    m, d = c2.shape
    n = ATTN_WIDTH
    tok = lambda i: (i, 0)
    modspec = lambda chunk: pl.BlockSpec((None, 1, d), lambda i: (ctx_row, 0, chunk))
    blk = tm * d * 4 * 2 + d * 2 * n * 2 * 2 + tm * n * 2 * 2 * 2
    return pl.pallas_call(
        _ctxproj_kernel,
        grid=(m // tm,),
        in_specs=[pl.BlockSpec((tm, d), tok), modspec(0), modspec(1),
                  pl.BlockSpec((1, d), lambda i: (0, 0)),
                  pl.BlockSpec((d, n), lambda i: (0, 2)),
                  pl.BlockSpec((n, d), lambda i: (0, 0))],
        out_specs=[pl.BlockSpec((tm, n), tok), pl.BlockSpec((n, tm), lambda i: (0, i))],
        out_shape=[jax.ShapeDtypeStruct((m, n), BF16), jax.ShapeDtypeStruct((n, m), BF16)],
        compiler_params=pltpu.CompilerParams(
            dimension_semantics=("arbitrary",), vmem_limit_bytes=_vmem_limit(blk)),
        name="ctxproj",
    )(c2, mod3, mod3, g_pre, w_in_bf, w_vt_bf)


def _attn_kernel(lam_ref, g_ref, q_ref, k_ref, kc_ref, vt_ref, vct_ref, w1_ref, w2_ref, w3_ref,
                 o_ref, w1_bf_ref, w2_bf_ref, w3_bf_ref, e1_ref, e2_ref, *, tq):
    for w_ref, w_bf_ref in ((w1_ref, w1_bf_ref), (w2_ref, w2_bf_ref), (w3_ref, w3_bf_ref)):
        w_bf_ref[...] = w_ref[...].astype(BF16)

    lp = lam_ref[...]
    lam = (jnp.exp(jnp.sum(lp[0:1] * lp[1:2], axis=-1, keepdims=True))
           - jnp.exp(jnp.sum(lp[2:3] * lp[3:4], axis=-1, keepdims=True)) + LAMBDA_INIT)
    k, kc = k_ref[...], kc_ref[...]
    vt = jnp.concatenate([vt_ref[...], vct_ref[...]], axis=1)
    vt = jnp.concatenate([vt, jnp.ones((SUM_ROWS, vt.shape[1]), BF16)], axis=0)
    gain = g_ref[...] * (1.0 - LAMBDA_INIT)

    def scores(i):
        q = q_ref[i * tq:(i + 1) * tq, :]
        lane = lax.broadcasted_iota(jnp.int32, q.shape, 1)
        zero = jnp.zeros_like(q)
        out = []
        for qm in (jnp.where(lane < QK_HEAD_DIM, q, zero), jnp.where(lane >= QK_HEAD_DIM, q, zero)):
            out.append((lax.dot_general(k, qm, NT_DIMS, preferred_element_type=F32),
                        lax.dot_general(kc, qm, NT_DIMS, preferred_element_type=F32)))
        return out

    n_lat = k_ref.shape[0]

    def weighted_values(e_ref, s, sc):
        m = jnp.maximum(jnp.max(s, axis=0, keepdims=True), jnp.max(sc, axis=0, keepdims=True))
        e_ref[0:n_lat, :] = jnp.exp2(s - m).astype(BF16)
        e_ref[n_lat:, :] = jnp.exp2(sc - m).astype(BF16)
        o = jnp.dot(vt, e_ref[...], preferred_element_type=F32)
        return o[:V_HEAD_DIM], o[V_HEAD_DIM:V_HEAD_DIM + 1]

    n_tiles = q_ref.shape[0] // tq
    nxt = scores(0)
    for i in range(n_tiles):
        cur = nxt
        if i + 1 < n_tiles:
            nxt = scores(i + 1)
        (o1, l1), (o2, l2) = weighted_values(e1_ref, *cur[0]), weighted_values(e2_ref, *cur[1])
        o = o1 * (1.0 / l1) - o2 * (lam / l2)
        o = o * lax.rsqrt(jnp.mean(o * o, axis=0, keepdims=True) + EPS)
        o_ref[i * tq:(i + 1) * tq, :] = (o.T * gain).astype(BF16)


def _attn(lam_params, g_subln, q, k, kc, vt, vct, later_weights, batch, seq, ctx_len, tq=512):
    m = q.shape[0]
    hd = V_HEAD_DIM
    n_steps = batch * N_HEADS
    full = lambda shape: pl.BlockSpec(shape, lambda b, h: (0, 0))
    rows = lambda n: pl.BlockSpec((n, hd), lambda b, h: (b, h))
    cols = lambda n: pl.BlockSpec((hd, n), lambda b, h: (h, b))
    slab = lambda w: pl.BlockSpec((w.shape[0] // n_steps, w.shape[1]), lambda b, h: (b * N_HEADS + h, 0))
    for w in later_weights:
        assert w.shape[0] % (n_steps * BF16_TILE_ROWS) == 0, w.shape
    w_bytes = sum(w.size // n_steps * (4 + 2) * 2 for w in later_weights)
    blk = (2 * seq + 2 * (seq + ctx_len)) * hd * 2 * 2 + 6 * tq * (seq + ctx_len) * 4 + w_bytes
    outs = pl.pallas_call(
        functools.partial(_attn_kernel, tq=tq),
        grid=(batch, N_HEADS),
        in_specs=[full((4, QK_HEAD_DIM)), full((1, hd)), rows(seq), rows(seq), rows(ctx_len),
                  cols(seq), cols(ctx_len)] + [slab(w) for w in later_weights],
        out_specs=[rows(seq)] + [slab(w) for w in later_weights],
        out_shape=[jax.ShapeDtypeStruct((m, ATTN_WIDTH), BF16)]
                  + [jax.ShapeDtypeStruct(w.shape, BF16) for w in later_weights],
        scratch_shapes=[pltpu.VMEM((seq + ctx_len, tq), BF16)] * 2,
        compiler_params=pltpu.CompilerParams(
            dimension_semantics=("arbitrary",) * 2, vmem_limit_bytes=_vmem_limit(blk)),
        name="attn",
    )(lam_params, g_subln, q, k, kc, vt, vct, *later_weights)
    return outs[0], outs[1:]


POOL_PAD = V7X_SUBLANES


def _pool_kernel(u_ref, w_ref, s_ref, o_ref, pad_ref):
    n = u_ref.shape[0]
    t = lax.broadcasted_iota(jnp.int32, (n, 1), 0)
    zeros = jnp.zeros((POOL_PAD, POOL_GROUP), F32)
    for g, win in enumerate(POOL_WINDOWS):
        cols = slice(g * POOL_GROUP, (g + 1) * POOL_GROUP)
        u = u_ref[:, cols]
        pad_ref[0:POOL_PAD, :] = zeros
        pad_ref[POOL_PAD:POOL_PAD + n, :] = u
        pad_ref[POOL_PAD + n:, :] = zeros
        acc = pad_ref[POOL_PAD - win // 2:POOL_PAD - win // 2 + n, :]
        for off in range(-win // 2 + 1, win // 2):
            acc = acc + pad_ref[POOL_PAD + off:POOL_PAD + off + n, :]
        cnt = (jnp.minimum(t + win // 2, n) - jnp.maximum(t - win // 2, 0)).astype(F32)
        d = (acc / cnt - u).astype(BF16)
        y = jnp.dot(d, w_ref[g].astype(BF16), preferred_element_type=F32)
        o_ref[:, cols] = (y * s_ref[:, cols]).astype(BF16)


def _pool(p, w_pool, pool_scale, batch, seq):
    m, n = p.shape
    blk = seq * n * (4 + 2) * 2 + w_pool.size * 4 * 2
    return pl.pallas_call(
        _pool_kernel,
        grid=(batch,),
        in_specs=[pl.BlockSpec((seq, n), lambda b: (b, 0)),
                  pl.BlockSpec(w_pool.shape, lambda b: (0, 0, 0)),
                  pl.BlockSpec((1, n), lambda b: (0, 0))],
        out_specs=pl.BlockSpec((seq, n), lambda b: (b, 0)),
        out_shape=jax.ShapeDtypeStruct((m, n), BF16),
        scratch_shapes=[pltpu.VMEM((seq + 2 * POOL_PAD, POOL_GROUP), F32)],
        compiler_params=pltpu.CompilerParams(
            dimension_semantics=("arbitrary",), vmem_limit_bytes=_vmem_limit(blk)),
        name="pool",
    )(p, w_pool, pool_scale)


OUTPROJ_SUBBLOCKS = 4
def _outproj_kernel(pool_ref, attn_ref, w_ref, x_ref, gt_ref, sh_ref, sc_ref, gpost_ref, gpre_ref,
                    xn_ref, h2_ref):
    sub = pool_ref.shape[0] // OUTPROJ_SUBBLOCKS
    parts = [slice(s * sub, (s + 1) * sub) for s in range(OUTPROJ_SUBBLOCKS)]
    mixes = [jnp.dot(pool_ref[r, :], w_ref[0:POOL_WIDTH, :], preferred_element_type=F32)
             + jnp.dot(attn_ref[r, :], w_ref[POOL_WIDTH:, :], preferred_element_type=F32) for r in parts]
    post_gain = gt_ref[...] * gpost_ref[...]
    pre_gain = gpre_ref[...] * (1.0 + sc_ref[...])
    for r, mix in zip(parts, mixes):
        xn = x_ref[r, :] + _rms(mix) * post_gain
        xn_ref[r, :] = xn
        h2_ref[r, :] = (_rms(xn) * pre_gain + sh_ref[...]).astype(BF16)


def _outproj(pool_o, attn_o, w_out_bf, x2, mod3, g_post, g_pre_ffn, seq, tm=512):
    m, d = x2.shape
    per_b = seq // tm
    tok = lambda i: (i, 0)
    modspec = lambda chunk: pl.BlockSpec((None, 1, d), lambda i: (i // per_b, 0, chunk))
    vec = pl.BlockSpec((1, d), lambda i: (0, 0))
    blk = (tm * d * 2 + tm * d * 4 * 2 + tm * d * 2) * 2 + d * d * 2
    return pl.pallas_call(
        _outproj_kernel,
        grid=(m // tm,),
        in_specs=[pl.BlockSpec((tm, POOL_WIDTH), tok), pl.BlockSpec((tm, ATTN_WIDTH), tok),
                  pl.BlockSpec((d, d), lambda i: (0, 0), pipeline_mode=pl.Buffered(1)),
                  pl.BlockSpec((tm, d), tok), modspec(2), modspec(3), modspec(4), vec, vec],
        out_specs=[pl.BlockSpec((tm, d), tok)] * 2,
        out_shape=[jax.ShapeDtypeStruct((m, d), F32), jax.ShapeDtypeStruct((m, d), BF16)],
        compiler_params=pltpu.CompilerParams(
            dimension_semantics=("arbitrary",), vmem_limit_bytes=_vmem_limit(blk)),
        name="outproj",
    )(pool_o, attn_o, w_out_bf, x2, mod3, mod3, mod3, g_post, g_pre_ffn)


def _ffn_kernel(h_ref, wg_ref, wu_ref, wd_ref, x_ref, gt_ref, gpost_ref, o_ref, acc_ref):
    j = pl.program_id(1)

    @pl.when(j == 0)
    def _():
        acc_ref[...] = jnp.zeros_like(acc_ref)

    h = h_ref[...]
    g = jnp.dot(h, wg_ref[...], preferred_element_type=F32)
    u = jnp.dot(h, wu_ref[...], preferred_element_type=F32)
    a = (_silu(g) * u).astype(BF16)
    acc_ref[...] += jnp.dot(a, wd_ref[...], preferred_element_type=F32)

    @pl.when(j == pl.num_programs(1) - 1)
    def _():
        o_ref[...] = x_ref[...] + gt_ref[...] * (_rms(acc_ref[...]) * gpost_ref[...])


def _ffn(h2, w_gu_bf, w_down_bf, xn, mod3, g_post, seq, tm=512, tf=512):
    m, d = h2.shape
    d_ff = w_down_bf.shape[0]
    nf = d_ff // tf
    per_b = seq // tm
    tok = lambda i, j: (i, 0)
    blk = (tm * d * 2 + tm * d * 4 * 2 + 3 * d * tf * 2) * 2 + tm * d * 4
    return pl.pallas_call(
        _ffn_kernel,
        grid=(m // tm, nf),
        in_specs=[pl.BlockSpec((tm, d), tok),
                  pl.BlockSpec((d, tf), lambda i, j: (0, j)),
                  pl.BlockSpec((d, tf), lambda i, j: (0, j + nf)),
                  pl.BlockSpec((tf, d), lambda i, j: (j, 0)),
                  pl.BlockSpec((tm, d), tok),
                  pl.BlockSpec((None, 1, d), lambda i, j: (i // per_b, 0, 5)),
                  pl.BlockSpec((1, d), lambda i, j: (0, 0))],
        out_specs=pl.BlockSpec((tm, d), tok),
        out_shape=jax.ShapeDtypeStruct((m, d), F32),
        scratch_shapes=[pltpu.VMEM((tm, d), F32)],
        compiler_params=pltpu.CompilerParams(
            dimension_semantics=("arbitrary", "arbitrary"), vmem_limit_bytes=_vmem_limit(blk)),
        name="ffn",
    )(h2, w_gu_bf, w_gu_bf, w_down_bf, xn, mod3, g_post)


def kernel(x, c, ctx, c_ctx, w_ada, b_ada, g_pre_mix, g_post_mix, g_pre_ffn, g_post_ffn, w_in, w_pool,
           pool_scale, lambda_q1, lambda_k1, lambda_q2, lambda_k2, g_subln, w_out, w_gate_up, w_down):
    batch, seq, d = x.shape
    ctx_len = ctx.shape[1]
    assert w_ada.shape[0] == 1, "single-layer block: the context stream is never updated"
    assert d == D_MODEL and batch + 1 <= 8

    s8 = jnp.concatenate([c, c_ctx[None, :], jnp.zeros((8 - batch - 1, d), F32)], axis=0)
    mod = _adaln(s8, w_ada[0], b_ada[0][None, :])
    mod3 = mod.reshape(8, 1, N_MOD * d)

    w_in_bf, w_vt_bf = _wprep(w_in[0], ATTN_WIDTH)

    x2 = x.reshape(batch * seq, d)
    c2 = ctx.reshape(batch * ctx_len, d)
    p, q, k, vt = _inproj(x2, mod3, g_pre_mix, w_in_bf, w_vt_bf, _rope_tables(seq), seq)
    kc, vct = _ctxproj(c2, mod3, batch, g_pre_mix, w_in_bf, w_vt_bf)

    lam_params = jnp.concatenate([lambda_q1, lambda_k1, lambda_q2, lambda_k2], axis=0)
    attn_o, (w_out_bf, w_gu_bf, w_down_bf) = _attn(
        lam_params, g_subln, q, k, kc, vt, vct, (w_out[0], w_gate_up[0], w_down[0]), batch, seq, ctx_len)
    pool_o = _pool(p, w_pool[0], pool_scale, batch, seq)

    xn, h2 = _outproj(pool_o, attn_o, w_out_bf, x2, mod3, g_post_mix, g_pre_ffn, seq)
    out = _ffn(h2, w_gu_bf, w_down_bf, xn, mod3, g_post_ffn, seq)
    return out.reshape(batch, seq, d)
```

```python
import functools
import math

import jax
import jax.numpy as jnp
import numpy as np
from jax import lax
from jax.experimental import pallas as pl
from jax.experimental.pallas import tpu as pltpu

D_MODEL = 2048
GRID_W = 64
POOL_WINDOWS = (2, 4, 8, 16)
POOL_WIDTH = D_MODEL // 2
POOL_GROUP = POOL_WIDTH // len(POOL_WINDOWS)
ATTN_WIDTH = D_MODEL - POOL_WIDTH
N_HEADS = 8
V_HEAD_DIM = ATTN_WIDTH // N_HEADS
QK_HEAD_DIM = V_HEAD_DIM // 2
ROPE_AXIS_DIM = QK_HEAD_DIM // 2
ROPE_BASE = 10000.0
N_MOD = 6
EPS = 1e-6
LAMBDA_INIT = 0.8 - 0.6 * math.exp(0.0)

V7X_LANES = 128
V7X_SUBLANES = 8
V7X_VMEM_BYTES = 64 * 1024 * 1024
V7X_VMEM_CAP = 60000 * 1024

F32 = jnp.float32
BF16 = jnp.bfloat16


def _vmem_limit(block_bytes):
    return int(min(V7X_VMEM_CAP, max(32 * 1024 * 1024, 2 * block_bytes)))


def _rms(x):
    return x * lax.rsqrt(jnp.mean(x * x, axis=-1, keepdims=True) + EPS)


def _silu(x):
    return x / (1.0 + jnp.exp(-x))


def _adaln_kernel(s_ref, w_ref, b_ref, o_ref):
    s = _silu(s_ref[...]).astype(BF16)
    o_ref[...] = jnp.dot(s, w_ref[...].astype(BF16), preferred_element_type=F32) + b_ref[...]


def _adaln(s8, w_ada, b_ada, tn=1024):
    d, n = w_ada.shape
    return pl.pallas_call(
        _adaln_kernel,
        grid=(n // tn,),
        in_specs=[pl.BlockSpec((8, d), lambda j: (0, 0)),
                  pl.BlockSpec((d, tn), lambda j: (0, j)),
                  pl.BlockSpec((1, tn), lambda j: (0, j))],
        out_specs=pl.BlockSpec((8, tn), lambda j: (0, j)),
        out_shape=jax.ShapeDtypeStruct((8, n), F32),
        compiler_params=pltpu.CompilerParams(
            dimension_semantics=("arbitrary",),
            vmem_limit_bytes=_vmem_limit(2 * d * tn * 4)),
        name="adaln",
    )(s8, w_ada, b_ada)


def _wprep_kernel(w_ref, wbf_ref, wvt_ref, *, n_plain):
    j = pl.program_id(0)

    @pl.when(j < n_plain)
    def _():
        wbf_ref[...] = w_ref[...].astype(BF16)

    @pl.when(j >= n_plain)
    def _():
        wvt_ref[...] = w_ref[...].T.astype(BF16)


def _wprep(w_in, n_value, tn=512):
    d, n = w_in.shape
    n_plain = (n - n_value) // tn
    return pl.pallas_call(
        functools.partial(_wprep_kernel, n_plain=n_plain),
        grid=(n // tn,),
        in_specs=[pl.BlockSpec((d, tn), lambda j: (0, j))],
        out_specs=[pl.BlockSpec((d, tn), lambda j: (0, jnp.minimum(j, n_plain - 1))),
                   pl.BlockSpec((tn, d), lambda j: (jnp.maximum(j - n_plain, 0), 0))],
        out_shape=[jax.ShapeDtypeStruct((d, n - n_value), BF16), jax.ShapeDtypeStruct((n_value, d), BF16)],
        compiler_params=pltpu.CompilerParams(
            dimension_semantics=("arbitrary",), vmem_limit_bytes=_vmem_limit(d * tn * (4 + 2 + 2) * 2)),
        name="wprep",
    )(w_in)


def _rope_tables(n_tokens):
    t = np.arange(n_tokens)
    half = ROPE_AXIS_DIM // 2
    inv = (np.float32(ROPE_BASE) ** (-np.arange(half, dtype=np.float32) / np.float32(half))).astype(np.float32)
    ang_r = (t // GRID_W).astype(np.float32)[:, None] * inv
    ang_c = (t % GRID_W).astype(np.float32)[:, None] * inv
    zero = np.zeros_like(ang_r)
    cos64 = np.concatenate([np.cos(ang_r), np.cos(ang_r), np.cos(ang_c), np.cos(ang_c)], axis=1)
    sa64 = np.concatenate([-np.sin(ang_r), zero, -np.sin(ang_c), zero], axis=1)
    sb64 = np.concatenate([zero, np.sin(ang_r), zero, np.sin(ang_c)], axis=1)
    rep = V7X_LANES // QK_HEAD_DIM
    return tuple(jnp.asarray(np.tile(a, (1, rep)).astype(np.float32)) for a in (cos64, sa64, sb64))


def _rope(x, cos, sa, sb):
    nxt = pltpu.roll(x, V7X_LANES - ROPE_AXIS_DIM // 2, 1)
    prv = pltpu.roll(x, ROPE_AXIS_DIM // 2, 1)
    return x * cos + nxt * sa + prv * sb


def _modulated_norm(x, g, sc, sh):
    return (_rms(x) * g) * (1.0 + sc) + sh


NT_DIMS = (((1,), (1,)), ((), ()))
BF16_TILE_ROWS = 2 * V7X_SUBLANES
SUM_ROWS = BF16_TILE_ROWS
LOG2_E = math.log2(math.e)


def _inproj_kernel(x_ref, sh_ref, sc_ref, g_ref, w_ref, wvt_ref, cos_ref, sa_ref, sb_ref,
                   p_ref, q_ref, k_ref, vt_ref):
    h = _modulated_norm(x_ref[...], g_ref[...], sc_ref[...], sh_ref[...]).astype(BF16)
    cos, sa, sb = cos_ref[...], sa_ref[...], sb_ref[...]
    n = POOL_WIDTH
    p_ref[...] = jnp.dot(h, w_ref[:, 0:n], preferred_element_type=F32)
    q = jnp.dot(h, w_ref[:, n:2 * n], preferred_element_type=F32)
    k = jnp.dot(h, w_ref[:, 2 * n:3 * n], preferred_element_type=F32)
    scale = QK_HEAD_DIM ** -0.5 * LOG2_E
    for c in range(n // V7X_LANES):
        sl = slice(c * V7X_LANES, (c + 1) * V7X_LANES)
        q_ref[:, sl] = (_rope(q[:, sl], cos, sa, sb) * scale).astype(BF16)
        k_ref[:, sl] = _rope(k[:, sl], cos, sa, sb).astype(BF16)
    vt_ref[...] = lax.dot_general(wvt_ref[...], h, NT_DIMS, preferred_element_type=F32).astype(BF16)


def _inproj(x2, mod3, g_pre, w_in_bf, w_vt_bf, tables, seq, tm=512):
    m, d = x2.shape
    n = POOL_WIDTH
    per_b = seq // tm
    tok = lambda i: (i, 0)
    modspec = lambda chunk: pl.BlockSpec((None, 1, d), lambda i: (i // per_b, 0, chunk))
    tabspec = pl.BlockSpec((tm, V7X_LANES), lambda i: (i % per_b, 0))
    blk = tm * d * 4 * 2 + d * 4 * n * 2 + tm * n * (4 + 2 + 2 + 2) * 2
    return pl.pallas_call(
        _inproj_kernel,
        grid=(m // tm,),
        in_specs=[pl.BlockSpec((tm, d), tok), modspec(0), modspec(1),
                  pl.BlockSpec((1, d), lambda i: (0, 0)),
                  pl.BlockSpec((d, 3 * n), lambda i: (0, 0), pipeline_mode=pl.Buffered(1)),
                  pl.BlockSpec((n, d), lambda i: (0, 0), pipeline_mode=pl.Buffered(1)),
                  tabspec, tabspec, tabspec],
        out_specs=[pl.BlockSpec((tm, n), tok)] * 3 + [pl.BlockSpec((n, tm), lambda i: (0, i))],
        out_shape=[jax.ShapeDtypeStruct((m, n), F32)] + [jax.ShapeDtypeStruct((m, n), BF16)] * 2
                  + [jax.ShapeDtypeStruct((n, m), BF16)],
        compiler_params=pltpu.CompilerParams(
            dimension_semantics=("arbitrary",), vmem_limit_bytes=_vmem_limit(blk)),
        name="inproj",
    )(x2, mod3, mod3, g_pre, w_in_bf, w_vt_bf, *tables)


def _ctxproj_kernel(x_ref, sh_ref, sc_ref, g_ref, wk_ref, wvt_ref, k_ref, vt_ref):
    h = _modulated_norm(x_ref[...], g_ref[...], sc_ref[...], sh_ref[...]).astype(BF16)
    k_ref[...] = jnp.dot(h, wk_ref[...], preferred_element_type=F32).astype(BF16)
    vt_ref[...] = lax.dot_general(wvt_ref[...], h, NT_DIMS, preferred_element_type=F32).astype(BF16)


def _ctxproj(c2, mod3, ctx_row, g_pre, w_in_bf, w_vt_bf, tm=512):
    m, d = c2.shape
    n = ATTN_WIDTH
    tok = lambda i: (i, 0)
    modspec = lambda chunk: pl.BlockSpec((None, 1, d), lambda i: (ctx_row, 0, chunk))
    blk = tm * d * 4 * 2 + d * 2 * n * 2 * 2 + tm * n * 2 * 2 * 2
    return pl.pallas_call(
        _ctxproj_kernel,
        grid=(m // tm,),
        in_specs=[pl.BlockSpec((tm, d), tok), modspec(0), modspec(1),
                  pl.BlockSpec((1, d), lambda i: (0, 0)),
                  pl.BlockSpec((d, n), lambda i: (0, 2)),
                  pl.BlockSpec((n, d), lambda i: (0, 0))],
        out_specs=[pl.BlockSpec((tm, n), tok), pl.BlockSpec((n, tm), lambda i: (0, i))],
        out_shape=[jax.ShapeDtypeStruct((m, n), BF16), jax.ShapeDtypeStruct((n, m), BF16)],
        compiler_params=pltpu.CompilerParams(
            dimension_semantics=("arbitrary",), vmem_limit_bytes=_vmem_limit(blk)),
        name="ctxproj",
    )(c2, mod3, mod3, g_pre, w_in_bf, w_vt_bf)


def _attn_kernel(lam_ref, g_ref, q_ref, k_ref, kc_ref, vt_ref, vct_ref, w1_ref, w2_ref, w3_ref,
                 o_ref, w1_bf_ref, w2_bf_ref, w3_bf_ref, e1_ref, e2_ref, *, tq):
    for w_ref, w_bf_ref in ((w1_ref, w1_bf_ref), (w2_ref, w2_bf_ref), (w3_ref, w3_bf_ref)):
        w_bf_ref[...] = w_ref[...].astype(BF16)

    lp = lam_ref[...]
    lam = (jnp.exp(jnp.sum(lp[0:1] * lp[1:2], axis=-1, keepdims=True))
           - jnp.exp(jnp.sum(lp[2:3] * lp[3:4], axis=-1, keepdims=True)) + LAMBDA_INIT)
    k, kc = k_ref[...], kc_ref[...]
    vt = jnp.concatenate([vt_ref[...], vct_ref[...]], axis=1)
    vt = jnp.concatenate([vt, jnp.ones((SUM_ROWS, vt.shape[1]), BF16)], axis=0)
    gain = g_ref[...] * (1.0 - LAMBDA_INIT)

    def scores(i):
        q = q_ref[i * tq:(i + 1) * tq, :]
        lane = lax.broadcasted_iota(jnp.int32, q.shape, 1)
        zero = jnp.zeros_like(q)
        out = []
        for qm in (jnp.where(lane < QK_HEAD_DIM, q, zero), jnp.where(lane >= QK_HEAD_DIM, q, zero)):
            out.append((lax.dot_general(k, qm, NT_DIMS, preferred_element_type=F32),
                        lax.dot_general(kc, qm, NT_DIMS, preferred_element_type=F32)))
        return out

    n_lat = k_ref.shape[0]

    def weighted_values(e_ref, s, sc):
        m = jnp.maximum(jnp.max(s, axis=0, keepdims=True), jnp.max(sc, axis=0, keepdims=True))
        e_ref[0:n_lat, :] = jnp.exp2(s - m).astype(BF16)
        e_ref[n_lat:, :] = jnp.exp2(sc - m).astype(BF16)
        o = jnp.dot(vt, e_ref[...], preferred_element_type=F32)
        return o[:V_HEAD_DIM], o[V_HEAD_DIM:V_HEAD_DIM + 1]

    n_tiles = q_ref.shape[0] // tq
    nxt = scores(0)
    for i in range(n_tiles):
        cur = nxt
        if i + 1 < n_tiles:
            nxt = scores(i + 1)
        (o1, l1), (o2, l2) = weighted_values(e1_ref, *cur[0]), weighted_values(e2_ref, *cur[1])
        o = o1 * (1.0 / l1) - o2 * (lam / l2)
        o = o * lax.rsqrt(jnp.mean(o * o, axis=0, keepdims=True) + EPS)
        o_ref[i * tq:(i + 1) * tq, :] = (o.T * gain).astype(BF16)


def _attn(lam_params, g_subln, q, k, kc, vt, vct, later_weights, batch, seq, ctx_len, tq=512):
    m = q.shape[0]
    hd = V_HEAD_DIM
    n_steps = batch * N_HEADS
    full = lambda shape: pl.BlockSpec(shape, lambda b, h: (0, 0))
    rows = lambda n: pl.BlockSpec((n, hd), lambda b, h: (b, h))
    cols = lambda n: pl.BlockSpec((hd, n), lambda b, h: (h, b))
    slab = lambda w: pl.BlockSpec((w.shape[0] // n_steps, w.shape[1]), lambda b, h: (b * N_HEADS + h, 0))
    for w in later_weights:
        assert w.shape[0] % (n_steps * BF16_TILE_ROWS) == 0, w.shape
    w_bytes = sum(w.size // n_steps * (4 + 2) * 2 for w in later_weights)
    blk = (2 * seq + 2 * (seq + ctx_len)) * hd * 2 * 2 + 6 * tq * (seq + ctx_len) * 4 + w_bytes
    outs = pl.pallas_call(
        functools.partial(_attn_kernel, tq=tq),
        grid=(batch, N_HEADS),
        in_specs=[full((4, QK_HEAD_DIM)), full((1, hd)), rows(seq), rows(seq), rows(ctx_len),
                  cols(seq), cols(ctx_len)] + [slab(w) for w in later_weights],
        out_specs=[rows(seq)] + [slab(w) for w in later_weights],
        out_shape=[jax.ShapeDtypeStruct((m, ATTN_WIDTH), BF16)]
                  + [jax.ShapeDtypeStruct(w.shape, BF16) for w in later_weights],
        scratch_shapes=[pltpu.VMEM((seq + ctx_len, tq), BF16)] * 2,
        compiler_params=pltpu.CompilerParams(
            dimension_semantics=("arbitrary",) * 2, vmem_limit_bytes=_vmem_limit(blk)),
        name="attn",
    )(lam_params, g_subln, q, k, kc, vt, vct, *later_weights)
    return outs[0], outs[1:]


POOL_PAD = max(POOL_WINDOWS)


def _window_sums(p, win):
    rows = p.shape[0]
    width = 1
    while 2 * width < win:
        p = p + pltpu.roll(p, rows - width, 0)
        width *= 2
    return p + pltpu.roll(p, win // 2, 0)


def _pool_kernel(u_ref, w_ref, s_ref, o_ref, pad_ref):
    n = u_ref.shape[0]
    edge = POOL_PAD
    t_head = lax.broadcasted_iota(jnp.int32, (edge, 1), 0)
    t_tail = t_head + (n - edge)
    zeros = jnp.zeros((POOL_PAD, POOL_GROUP), F32)
    pad_ref[0:POOL_PAD, :] = zeros
    pad_ref[POOL_PAD + n:, :] = zeros
    for g, win in enumerate(POOL_WINDOWS):
        cols = slice(g * POOL_GROUP, (g + 1) * POOL_GROUP)
        u = u_ref[:, cols]
        pad_ref[POOL_PAD:POOL_PAD + n, :] = u
        acc = _window_sums(pad_ref[...], win)[POOL_PAD:POOL_PAD + n]
        count = lambda t: (jnp.minimum(t + win // 2, n) - jnp.maximum(t - win // 2, 0)).astype(F32)
        mean = jnp.concatenate([acc[:edge] / count(t_head), acc[edge:n - edge] * (1.0 / win),
                                acc[n - edge:] / count(t_tail)], axis=0)
        d = (mean - u).astype(BF16)
        y = jnp.dot(d, w_ref[g].astype(BF16), preferred_element_type=F32)
        o_ref[:, cols] = (y * s_ref[:, cols]).astype(BF16)


def _pool(p, w_pool, pool_scale, batch, seq):
    m, n = p.shape
    assert seq > 2 * POOL_PAD
    blk = seq * n * (4 + 2) * 2 + w_pool.size * 4 * 2
    return pl.pallas_call(
        _pool_kernel,
        grid=(batch,),
        in_specs=[pl.BlockSpec((seq, n), lambda b: (b, 0)),
                  pl.BlockSpec(w_pool.shape, lambda b: (0, 0, 0)),
                  pl.BlockSpec((1, n), lambda b: (0, 0))],
        out_specs=pl.BlockSpec((seq, n), lambda b: (b, 0)),
        out_shape=jax.ShapeDtypeStruct((m, n), BF16),
        scratch_shapes=[pltpu.VMEM((seq + 2 * POOL_PAD, POOL_GROUP), F32)],
        compiler_params=pltpu.CompilerParams(
            dimension_semantics=("arbitrary",), vmem_limit_bytes=_vmem_limit(blk)),
        name="pool",
    )(p, w_pool, pool_scale)


OUTPROJ_SUBBLOCKS = 4
def _outproj_kernel(pool_ref, attn_ref, w_ref, x_ref, gt_ref, sh_ref, sc_ref, gpost_ref, gpre_ref,
                    xn_ref, h2_ref):
    sub = pool_ref.shape[0] // OUTPROJ_SUBBLOCKS
    parts = [slice(s * sub, (s + 1) * sub) for s in range(OUTPROJ_SUBBLOCKS)]
    mixes = [jnp.dot(pool_ref[r, :], w_ref[0:POOL_WIDTH, :], preferred_element_type=F32)
             + jnp.dot(attn_ref[r, :], w_ref[POOL_WIDTH:, :], preferred_element_type=F32) for r in parts]
    post_gain = gt_ref[...] * gpost_ref[...]
    pre_gain = gpre_ref[...] * (1.0 + sc_ref[...])
    for r, mix in zip(parts, mixes):
        xn = x_ref[r, :] + _rms(mix) * post_gain
        xn_ref[r, :] = xn
        h2_ref[r, :] = (_rms(xn) * pre_gain + sh_ref[...]).astype(BF16)


def _outproj(pool_o, attn_o, w_out_bf, x2, mod3, g_post, g_pre_ffn, seq, tm=512):
    m, d = x2.shape
    per_b = seq // tm
    tok = lambda i: (i, 0)
    modspec = lambda chunk: pl.BlockSpec((None, 1, d), lambda i: (i // per_b, 0, chunk))
    vec = pl.BlockSpec((1, d), lambda i: (0, 0))
    blk = (tm * d * 2 + tm * d * 4 * 2 + tm * d * 2) * 2 + d * d * 2
    return pl.pallas_call(
        _outproj_kernel,
        grid=(m // tm,),
        in_specs=[pl.BlockSpec((tm, POOL_WIDTH), tok), pl.BlockSpec((tm, ATTN_WIDTH), tok),
                  pl.BlockSpec((d, d), lambda i: (0, 0), pipeline_mode=pl.Buffered(1)),
                  pl.BlockSpec((tm, d), tok), modspec(2), modspec(3), modspec(4), vec, vec],
        out_specs=[pl.BlockSpec((tm, d), tok)] * 2,
        out_shape=[jax.ShapeDtypeStruct((m, d), F32), jax.ShapeDtypeStruct((m, d), BF16)],
        compiler_params=pltpu.CompilerParams(
            dimension_semantics=("arbitrary",), vmem_limit_bytes=_vmem_limit(blk)),
        name="outproj",
    )(pool_o, attn_o, w_out_bf, x2, mod3, mod3, mod3, g_post, g_pre_ffn)


def _ffn_kernel(h_ref, wg_ref, wu_ref, wd_ref, x_hbm, gt_ref, gpost_ref, o_ref, acc_ref, x_sem):
    i, j = pl.program_id(0), pl.program_id(1)
    tm = o_ref.shape[0]

    def residual_copy():
        rows = pl.ds(pl.multiple_of(i * tm, tm), tm)
        return pltpu.make_async_copy(x_hbm.at[rows, :], o_ref, x_sem)

    @pl.when(j == 0)
    def _():
        residual_copy().start()
        acc_ref[...] = jnp.zeros_like(acc_ref)

    h = h_ref[...]
    g = jnp.dot(h, wg_ref[...], preferred_element_type=F32)
    u = jnp.dot(h, wu_ref[...], preferred_element_type=F32)
    a = (_silu(g) * u).astype(BF16)
    acc_ref[...] += jnp.dot(a, wd_ref[...], preferred_element_type=F32)

    @pl.when(j == pl.num_programs(1) - 1)
    def _():
        residual_copy().wait()
        gain = gt_ref[...] * gpost_ref[...]
        o_ref[...] = o_ref[...] + _rms(acc_ref[...]) * gain


def _ffn(h2, w_gu_bf, w_down_bf, xn, mod3, g_post, seq, tm=1024, tf=512):
    m, d = h2.shape
    d_ff = w_down_bf.shape[0]
    nf = d_ff // tf
    per_b = seq // tm
    tok = lambda i, j: (i, 0)
    blk = (tm * d * 2 + tm * d * 4 + 3 * d * tf * 2) * 2 + tm * d * 4
    return pl.pallas_call(
        _ffn_kernel,
        grid=(m // tm, nf),
        in_specs=[pl.BlockSpec((tm, d), tok),
                  pl.BlockSpec((d, tf), lambda i, j: (0, j)),
                  pl.BlockSpec((d, tf), lambda i, j: (0, j + nf)),
                  pl.BlockSpec((tf, d), lambda i, j: (j, 0)),
                  pl.BlockSpec(memory_space=pl.ANY),
                  pl.BlockSpec((None, 1, d), lambda i, j: (i // per_b, 0, 5)),
                  pl.BlockSpec((1, d), lambda i, j: (0, 0))],
        out_specs=pl.BlockSpec((tm, d), tok),
        out_shape=jax.ShapeDtypeStruct((m, d), F32),
        scratch_shapes=[pltpu.VMEM((tm, d), F32), pltpu.SemaphoreType.DMA(())],
        compiler_params=pltpu.CompilerParams(
            dimension_semantics=("arbitrary", "arbitrary"), vmem_limit_bytes=_vmem_limit(blk)),
        name="ffn",
    )(h2, w_gu_bf, w_gu_bf, w_down_bf, xn, mod3, g_post)


def kernel(x, c, ctx, c_ctx, w_ada, b_ada, g_pre_mix, g_post_mix, g_pre_ffn, g_post_ffn, w_in, w_pool,
           pool_scale, lambda_q1, lambda_k1, lambda_q2, lambda_k2, g_subln, w_out, w_gate_up, w_down):
    batch, seq, d = x.shape
    ctx_len = ctx.shape[1]
    assert w_ada.shape[0] == 1, "single-layer block: the context stream is never updated"
    assert d == D_MODEL and batch + 1 <= 8

    s8 = jnp.concatenate([c, c_ctx[None, :], jnp.zeros((8 - batch - 1, d), F32)], axis=0)
    mod = _adaln(s8, w_ada[0], b_ada[0][None, :])
    mod3 = mod.reshape(8, 1, N_MOD * d)

    w_in_bf, w_vt_bf = _wprep(w_in[0], ATTN_WIDTH)

    x2 = x.reshape(batch * seq, d)
    c2 = ctx.reshape(batch * ctx_len, d)
    p, q, k, vt = _inproj(x2, mod3, g_pre_mix, w_in_bf, w_vt_bf, _rope_tables(seq), seq)
    kc, vct = _ctxproj(c2, mod3, batch, g_pre_mix, w_in_bf, w_vt_bf)

    lam_params = jnp.concatenate([lambda_q1, lambda_k1, lambda_q2, lambda_k2], axis=0)
    attn_o, (w_out_bf, w_gu_bf, w_down_bf) = _attn(
        lam_params, g_subln, q, k, kc, vt, vct, (w_out[0], w_gate_up[0], w_down[0]), batch, seq, ctx_len)
    pool_o = _pool(p, w_pool[0], pool_scale, batch, seq)

    xn, h2 = _outproj(pool_o, attn_o, w_out_bf, x2, mod3, g_post_mix, g_pre_ffn, seq)
    out = _ffn(h2, w_gu_bf, w_down_bf, xn, mod3, g_post_ffn, seq)
    return out.reshape(batch, seq, d)
```
